```python
import jax, jax.numpy as jnp
from jax import lax
import numpy as np

D_MODEL = 2048
BATCH = 2
SEQ = 8192
DEPTH = 1

CHUNK = 64
MEM_LEN = 256
D_MIX = D_MODEL
D_POOL = D_MIX // 2
POOL_WINDOWS = (2, 4, 8, 16)
N_POOL_GROUPS = len(POOL_WINDOWS)
POOL_GROUP = D_POOL // N_POOL_GROUPS
D_GLA = D_MIX - D_POOL
GLA_HEADS = 4
GLA_DV = D_GLA // GLA_HEADS
GLA_DK = GLA_DV // 2
D_GLA_K = GLA_HEADS * GLA_DK
GLA_GATE_RANK = 16
GLA_GATE_TEMP = 16.0
XATTN_HEADS = 4
XATTN_HEAD_DIM = D_MODEL // XATTN_HEADS
D_FF = ((8 * D_MODEL // 3 + 255) // 256) * 256
RMS_EPS = 1e-6
IN_SPLITS = (D_POOL,
             D_POOL + D_GLA_K,
             D_POOL + 2 * D_GLA_K,
             D_POOL + 2 * D_GLA_K + D_GLA,
             D_POOL + 2 * D_GLA_K + 2 * D_GLA)
IN_COLS = D_POOL + 2 * D_GLA_K + 2 * D_GLA + GLA_GATE_RANK

kernel_name = "hymba_pool_gla_macaron_memxattn"


def rms_norm(x, gain):
    xf = x.astype(jnp.float32)
    y = xf * lax.rsqrt(jnp.mean(xf * xf, axis=-1, keepdims=True) + RMS_EPS)
    return (y * gain.astype(jnp.float32)).astype(x.dtype)


def swiglu(h, w_gate, w_up, w_down):
    return (jax.nn.silu(h @ w_gate) * (h @ w_up)) @ w_down


def pool_mixer(u, pool_w, pool_scale):
    b, s, _ = u.shape
    uf = u.astype(jnp.float32).reshape(b, s, N_POOL_GROUPS, POOL_GROUP)
    cs = jnp.cumsum(uf, axis=1)
    count = jnp.arange(1, s + 1, dtype=jnp.float32)
    diffs = []
    for g, w in enumerate(POOL_WINDOWS):
        cs_g = cs[:, :, g]
        lagged = jnp.pad(cs_g, ((0, 0), (w, 0), (0, 0)))[:, :s]
        mean = (cs_g - lagged) / jnp.minimum(count, float(w))[None, :, None]
        diffs.append(mean - uf[:, :, g])
    d = jnp.stack(diffs, axis=2).astype(u.dtype)
    y = jnp.einsum('bsgc,gcd->bsgd', d, pool_w)
    return y.reshape(b, s, D_POOL) * pool_scale


def gla_mixer(q, k, v, g, a_lr, w_a2, b_a, head_norm):
    b, s, _ = q.shape
    nc = s // CHUNK
    dt = v.dtype
    log_a = jax.nn.log_sigmoid((a_lr @ w_a2 + b_a).astype(jnp.float32)) / GLA_GATE_TEMP

    def chunks(t, d):
        return t.astype(jnp.float32).reshape(b, nc, CHUNK, GLA_HEADS, d)

    qc = chunks(q, GLA_DK) * (GLA_DK ** -0.5)
    kc = chunks(k, GLA_DK)
    vc = chunks(v, GLA_DV)
    cum = jnp.cumsum(chunks(log_a, GLA_DK), axis=2)
    b_end = cum[:, :, -1:]
    k_dec = kc * jnp.exp(b_end - cum)
    q_dec = qc * jnp.exp(b_end)
    scores = jnp.einsum('bnihk,bnjhk->bnhij', qc, k_dec)
    o_intra = jnp.einsum('bnhij,bnjhv->bnihv', scores, vc)

    def step(state, xs):
        q_c, k_c, v_c, decay_c = xs
        o = jnp.einsum('bihk,bhkv->bihv', q_c, state)
        state = decay_c[..., None] * state + jnp.einsum('bjhk,bjhv->bhkv', k_c, v_c)
        return state, o

    xs = (jnp.moveaxis(q_dec, 1, 0), jnp.moveaxis(k_dec, 1, 0),
          jnp.moveaxis(vc, 1, 0), jnp.moveaxis(jnp.exp(b_end[:, :, 0]), 1, 0))
    s0 = jnp.zeros((b, GLA_HEADS, GLA_DK, GLA_DV), jnp.float32)
    _, o_inter = lax.scan(step, s0, xs)
    o = o_intra + jnp.moveaxis(o_inter, 0, 1)
    o = o * lax.rsqrt(jnp.mean(o * o, axis=-1, keepdims=True) + RMS_EPS)
    o = o.reshape(b, s, D_GLA) * head_norm.astype(jnp.float32)
    return o.astype(dt) * jax.nn.silu(g)


def mem_cross_attention(h, mem_h, w_q, w_kv, w_o):
    b, s, _ = h.shape
    q = (h @ w_q).reshape(b, s, XATTN_HEADS, XATTN_HEAD_DIM)
    kv = (mem_h @ w_kv).reshape(b, mem_h.shape[1], 2, XATTN_HEADS, XATTN_HEAD_DIM)
    k, v = kv[:, :, 0], kv[:, :, 1]
    logits = jnp.einsum('bshd,bmhd->bhsm', q, k).astype(jnp.float32) * (XATTN_HEAD_DIM ** -0.5)
    p = jax.nn.softmax(logits, axis=-1).astype(v.dtype)
    o = jnp.einsum('bhsm,bmhd->bshd', p, v).reshape(b, s, D_MODEL)
    return o @ w_o


def setup_inputs(seed: int = 0) -> dict:
    key = jax.random.key(seed)
    ks = jax.random.split(key, 32)

    def dense(k, shape, fan_in):
        return jax.random.normal(k, shape, jnp.float32) * (fan_in ** -0.5)

    def gain(k, shape):
        return 1.0 + 0.02 * jax.random.normal(k, shape, jnp.float32)

    L = DEPTH
    return {
        "x": jax.random.normal(ks[0], (BATCH, SEQ, D_MODEL), jnp.float32),
        "mem": jax.random.normal(ks[1], (BATCH, MEM_LEN, D_MODEL), jnp.float32),
        "ffn1_norm": gain(ks[2], (L, D_MODEL)),
        "ffn1_w_gate": dense(ks[3], (L, D_MODEL, D_FF), D_MODEL),
        "ffn1_w_up": dense(ks[4], (L, D_MODEL, D_FF), D_MODEL),
        "ffn1_w_down": dense(ks[5], (L, D_FF, D_MODEL), D_FF),
        "mix_norm": gain(ks[6], (L, D_MODEL)),
        "w_in": dense(ks[7], (L, D_MODEL, IN_COLS), D_MODEL),
        "pool_w": dense(ks[8], (L, N_POOL_GROUPS, POOL_GROUP, POOL_GROUP), POOL_GROUP),
        "pool_scale": 1.0 + 0.1 * jax.random.normal(ks[9], (L, D_POOL), jnp.float32),
        "gla_w_a2": dense(ks[10], (L, GLA_GATE_RANK, D_GLA_K), GLA_GATE_RANK),
        "gla_b_a": 0.1 * jax.random.normal(ks[11], (L, D_GLA_K), jnp.float32),
        "gla_head_norm": gain(ks[12], (L, D_GLA)),
        "w_out": dense(ks[13], (L, D_MIX, D_MODEL), D_MIX),
        "xattn_norm": gain(ks[14], (L, D_MODEL)),
        "mem_norm": gain(ks[15], (L, D_MODEL)),
        "xattn_w_q": dense(ks[16], (L, D_MODEL, D_MODEL), D_MODEL),
        "xattn_w_kv": dense(ks[17], (L, D_MODEL, 2 * D_MODEL), D_MODEL),
        "xattn_w_o": dense(ks[18], (L, D_MODEL, D_MODEL), D_MODEL),
        "ffn2_norm": gain(ks[19], (L, D_MODEL)),
        "ffn2_w_gate": dense(ks[20], (L, D_MODEL, D_FF), D_MODEL),
        "ffn2_w_up": dense(ks[21], (L, D_MODEL, D_FF), D_MODEL),
        "ffn2_w_down": dense(ks[22], (L, D_FF, D_MODEL), D_FF),
        "final_norm": gain(ks[23], (D_MODEL,)),
    }


def reference(x, mem, ffn1_norm, ffn1_w_gate, ffn1_w_up, ffn1_w_down, mix_norm, w_in,
              pool_w, pool_scale, gla_w_a2, gla_b_a, gla_head_norm, w_out,
              xattn_norm, mem_norm, xattn_w_q, xattn_w_kv, xattn_w_o,
              ffn2_norm, ffn2_w_gate, ffn2_w_up, ffn2_w_down, final_norm):
    for l in range(DEPTH):
        x = x + 0.5 * swiglu(rms_norm(x, ffn1_norm[l]), ffn1_w_gate[l], ffn1_w_up[l], ffn1_w_down[l])
        h = rms_norm(x, mix_norm[l])
        proj = h @ w_in[l]
        u, q, k, v, g, a_lr = jnp.split(proj, list(IN_SPLITS), axis=-1)
        y_pool = pool_mixer(u, pool_w[l], pool_scale[l])
        y_gla = gla_mixer(q, k, v, g, a_lr, gla_w_a2[l], gla_b_a[l], gla_head_norm[l])
        x = x + jnp.concatenate([y_pool, y_gla], axis=-1) @ w_out[l]
        x = x + mem_cross_attention(rms_norm(x, xattn_norm[l]), rms_norm(mem, mem_norm[l]),
                                    xattn_w_q[l], xattn_w_kv[l], xattn_w_o[l])
        x = x + 0.5 * swiglu(rms_norm(x, ffn2_norm[l]), ffn2_w_gate[l], ffn2_w_up[l], ffn2_w_down[l])
    return rms_norm(x, final_norm)
```

```python
import functools

import jax
import jax.numpy as jnp
from jax import lax
from jax.experimental import pallas as pl
from jax.experimental.pallas import tpu as pltpu

F32 = jnp.float32
BF16 = jnp.bfloat16

D_MODEL = 2048
CHUNK = 64
D_POOL = 1024
POOL_WINDOWS = (2, 4, 8, 16)
POOL_GROUP = 256
POOL_HALO = 16
D_GLA = 1024
GLA_HEADS = 4
GLA_DV = 256
GLA_DK = 128
D_GLA_K = GLA_HEADS * GLA_DK
GLA_GATE_RANK = 16
GLA_GATE_TEMP = 16.0
XATTN_HEADS = 4
XATTN_HEAD_DIM = D_MODEL // XATTN_HEADS
RMS_EPS = 1e-6
LANES = 128

Q_OFF = D_POOL
K_OFF = Q_OFF + D_GLA_K
V_OFF = K_OFF + D_GLA_K
G_OFF = V_OFF + D_GLA
MAIN_COLS = G_OFF + D_GLA

VMEM_LIMIT_BYTES = 56 * 1024 * 1024

FFN_TM = 512
FFN_TF = 512
MIX_TM = 256
XATTN_TM = 256
KV_TN = 1024


def _rms(x, gain):
    return x * lax.rsqrt(jnp.mean(x * x, axis=-1, keepdims=True) + RMS_EPS) * gain


def _dot(a, b):
    return jnp.dot(a, b, preferred_element_type=F32)


def _resident(shape):
    return pl.BlockSpec(shape, lambda *_: (0,) * len(shape), pipeline_mode=pl.Buffered(1))


def _ffn_kernel(x_ref, gain_ref, wg_ref, wu_ref, wd_ref, fgain_ref, o_ref, h_ref, *, final_norm):
    j = pl.program_id(1)
    last = pl.num_programs(1) - 1

    @pl.when(j == 0)
    def _():
        h_ref[...] = _rms(x_ref[...], gain_ref[...]).astype(BF16)

    h = h_ref[...]
    g = _dot(h, wg_ref[...])
    u = _dot(h, wu_ref[...])
    a = (g * jax.nn.sigmoid(g) * u).astype(BF16)
    part = _dot(a, wd_ref[...])

    @pl.when(j == 0)
    def _():
        o_ref[...] = part

    @pl.when(j > 0)
    def _():
        o_ref[...] += part

    @pl.when(j == last)
    def _():
        y = x_ref[...] + 0.5 * o_ref[...]
        if final_norm:
            y = _rms(y, fgain_ref[...])
        o_ref[...] = y


def _ffn(x, gain, wg, wu, wd, fgain, *, final_norm):
    t, d = x.shape
    dff = wg.shape[1]
    grid = (t // FFN_TM, dff // FFN_TF)
    return pl.pallas_call(
        functools.partial(_ffn_kernel, final_norm=final_norm),
        grid=grid,
        in_specs=[
            pl.BlockSpec((FFN_TM, d), lambda i, j: (i, 0)),
            pl.BlockSpec((1, d), lambda i, j: (0, 0)),
            pl.BlockSpec((d, FFN_TF), lambda i, j: (0, j)),
            pl.BlockSpec((d, FFN_TF), lambda i, j: (0, j)),
            pl.BlockSpec((FFN_TF, d), lambda i, j: (j, 0)),
            pl.BlockSpec((1, d), lambda i, j: (0, 0)),
        ],
        out_specs=pl.BlockSpec((FFN_TM, d), lambda i, j: (i, 0)),
        out_shape=jax.ShapeDtypeStruct((t, d), F32),
        scratch_shapes=[pltpu.VMEM((FFN_TM, d), BF16)],
        compiler_params=pltpu.CompilerParams(
            dimension_semantics=("parallel", "arbitrary"),
            vmem_limit_bytes=VMEM_LIMIT_BYTES),
        name="ffn_final" if final_norm else "ffn",
    )(x, gain, wg, wu, wd, fgain)


def _mix_kernel(x_ref, gain_ref, win_ref, wa_ref, wa2_ref, ba_ref, poolw_ref, pscale_ref,
                hnorm_ref, wout_ref, o_ref,
                proj_ref, gate_ref, ubuf_ref, cat_ref, state_ref):
    tm = x_ref.shape[0]
    s = pl.program_id(1)

    @pl.when(s == 0)
    def _():
        state_ref[...] = jnp.zeros_like(state_ref)
        ubuf_ref[0:POOL_HALO, :] = jnp.zeros((POOL_HALO, D_POOL), F32)

    h = _rms(x_ref[...], gain_ref[...]).astype(BF16)
    proj_ref[...] = _dot(h, win_ref[...])
    a_lr = _dot(h, wa_ref[...])
    z = _dot(a_lr.astype(BF16), wa2_ref[...]) + ba_ref[...]
    gate_ref[...] = (jnp.minimum(z, 0.0) - jnp.log1p(jnp.exp(-jnp.abs(z)))) * (1.0 / GLA_GATE_TEMP)

    ubuf_ref[POOL_HALO:POOL_HALO + tm, :] = proj_ref[:, 0:D_POOL]
    t_idx = s * tm + lax.broadcasted_iota(jnp.int32, (tm, 1), 0)
    count = (t_idx + 1).astype(F32)
    for gi, w in enumerate(POOL_WINDOWS):
        cols = slice(gi * POOL_GROUP, (gi + 1) * POOL_GROUP)
        ext = ubuf_ref[:, cols]
        acc = ext
        shift = 1
        while shift < w:
            acc = acc + pltpu.roll(acc, shift, 0)
            shift *= 2
        win = acc[POOL_HALO:, :]
        d = win / jnp.minimum(count, float(w)) - ext[POOL_HALO:, :]
        y = _dot(d.astype(BF16), poolw_ref[gi]) * pscale_ref[:, cols]
        cat_ref[:, cols] = y.astype(BF16)
    ubuf_ref[0:POOL_HALO, :] = ubuf_ref[tm:tm + POOL_HALO, :]

    row = lax.broadcasted_iota(jnp.int32, (CHUNK, CHUNK), 0)
    col = lax.broadcasted_iota(jnp.int32, (CHUNK, CHUNK), 1)
    tri = (row >= col).astype(BF16)
    q_scale = GLA_DK ** -0.5

    def chunk_body(c, carry):
        r0 = pl.multiple_of(c * CHUNK, CHUNK)
        rows = pl.ds(r0, CHUNK)
        la = gate_ref[rows, :]
        la_hi = la.astype(BF16)
        la_lo = (la - la_hi.astype(F32)).astype(BF16)
        cum = _dot(tri, la_hi) + _dot(tri, la_lo)
        b_end = cum[CHUNK - 1:CHUNK, :]
        k_dec = proj_ref[rows, K_OFF:K_OFF + D_GLA_K] * jnp.exp(b_end - cum)
        decay = jnp.exp(b_end)
        q = proj_ref[rows, Q_OFF:Q_OFF + D_GLA_K] * q_scale
        for hd in range(GLA_HEADS):
            kc = slice(hd * GLA_DK, (hd + 1) * GLA_DK)
            vc = slice(V_OFF + hd * GLA_DV, V_OFF + (hd + 1) * GLA_DV)
            gc = slice(G_OFF + hd * GLA_DV, G_OFF + (hd + 1) * GLA_DV)
            v_h = proj_ref[rows, vc].astype(BF16)
            kd_h = k_dec[:, kc].astype(BF16)
            upd = lax.dot_general(v_h, kd_h, (((0,), (0,)), ((), ())),
                                  preferred_element_type=F32)
            st = state_ref[hd] * decay[:, kc] + upd
            state_ref[hd] = st
            o = lax.dot_general(q[:, kc].astype(BF16), st.astype(BF16), (((1,), (1,)), ((), ())),
                                preferred_element_type=F32)
            o = o * lax.rsqrt(jnp.mean(o * o, axis=-1, keepdims=True) + RMS_EPS)
            o = o * hnorm_ref[:, hd * GLA_DV:(hd + 1) * GLA_DV]
            g_h = proj_ref[rows, gc]
            y = o * (g_h * jax.nn.sigmoid(g_h))
            cat_ref[rows, D_POOL + hd * GLA_DV:D_POOL + (hd + 1) * GLA_DV] = y.astype(BF16)
        return carry

    lax.fori_loop(0, tm // CHUNK, chunk_body, 0)

    o_ref[...] = x_ref[...] + _dot(cat_ref[...], wout_ref[...])


def _mixer(x, gain, win, wa, wa2, ba, poolw, pscale, hnorm, wout):
    b, s, d = x.shape
    tm = MIX_TM
    return pl.pallas_call(
        _mix_kernel,
        grid=(b, s // tm),
        in_specs=[
            pl.BlockSpec((None, tm, d), lambda i, j: (i, j, 0)),
            _resident((1, d)),
            _resident((d, MAIN_COLS)),
            _resident((d, LANES)),
            _resident((LANES, D_GLA_K)),
            _resident((1, D_GLA_K)),
            _resident((len(POOL_WINDOWS), POOL_GROUP, POOL_GROUP)),
            _resident((1, D_POOL)),
            _resident((1, D_GLA)),
            _resident((d, d)),
        ],
        out_specs=pl.BlockSpec((None, tm, d), lambda i, j: (i, j, 0)),
        out_shape=jax.ShapeDtypeStruct((b, s, d), F32),
        scratch_shapes=[
            pltpu.VMEM((tm, MAIN_COLS), F32),
            pltpu.VMEM((tm, D_GLA_K), F32),
            pltpu.VMEM((POOL_HALO + tm, D_POOL), F32),
            pltpu.VMEM((tm, d), BF16),
            pltpu.VMEM((GLA_HEADS, GLA_DV, GLA_DK), F32),
        ],
        compiler_params=pltpu.CompilerParams(
            dimension_semantics=("arbitrary", "arbitrary"),
            vmem_limit_bytes=VMEM_LIMIT_BYTES),
        name="mixer",
    )(x, gain, win, wa, wa2, ba, poolw, pscale, hnorm, wout)


def _kv_kernel(mem_ref, gain_ref, wkv_ref, o_ref):
    h = _rms(mem_ref[...], gain_ref[...]).astype(BF16)
    o_ref[...] = _dot(h, wkv_ref[...]).astype(BF16)


def _mem_kv(mem, gain, wkv):
    t, d = mem.shape
    n = wkv.shape[1]
    return pl.pallas_call(
        _kv_kernel,
        grid=(n // KV_TN,),
        in_specs=[
            pl.BlockSpec((t, d), lambda j: (0, 0)),
            pl.BlockSpec((1, d), lambda j: (0, 0)),
            pl.BlockSpec((d, KV_TN), lambda j: (0, j)),
        ],
        out_specs=pl.BlockSpec((t, KV_TN), lambda j: (0, j)),
        out_shape=jax.ShapeDtypeStruct((t, n), BF16),
        compiler_params=pltpu.CompilerParams(
            dimension_semantics=("arbitrary",),
            vmem_limit_bytes=VMEM_LIMIT_BYTES),
        name="mem_kv",
    )(mem, gain, wkv)


def _xattn_kernel(x_ref, gain_ref, wq_ref, kv_ref, wo_ref, o_ref, cat_ref):
    h = _rms(x_ref[...], gain_ref[...]).astype(BF16)
    q = (_dot(h, wq_ref[...]) * (XATTN_HEAD_DIM ** -0.5)).astype(BF16)
    for hd in range(XATTN_HEADS):
        cols = slice(hd * XATTN_HEAD_DIM, (hd + 1) * XATTN_HEAD_DIM)
        k_h = kv_ref[:, cols]
        v_h = kv_ref[:, D_MODEL + hd * XATTN_HEAD_DIM:D_MODEL + (hd + 1) * XATTN_HEAD_DIM]
        logits = lax.dot_general(q[:, cols], k_h, (((1,), (1,)), ((), ())),
                                 preferred_element_type=F32)
        m = jnp.max(logits, axis=-1, keepdims=True)
        e = jnp.exp(logits - m)
        p = e / jnp.sum(e, axis=-1, keepdims=True)
        cat_ref[:, cols] = _dot(p.astype(BF16), v_h).astype(BF16)
    o_ref[...] = x_ref[...] + _dot(cat_ref[...], wo_ref[...])


def _xattn(x, gain, wq, kv, wo):
    b, s, d = x.shape
    m = kv.shape[1]
    tm = XATTN_TM
    return pl.pallas_call(
        _xattn_kernel,
        grid=(b, s // tm),
        in_specs=[
            pl.BlockSpec((None, tm, d), lambda i, j: (i, j, 0)),
            _resident((1, d)),
            _resident((d, d)),
            pl.BlockSpec((None, m, 2 * d), lambda i, j: (i, 0, 0)),
            _resident((d, d)),
        ],
        out_specs=pl.BlockSpec((None, tm, d), lambda i, j: (i, j, 0)),
        out_shape=jax.ShapeDtypeStruct((b, s, d), F32),
        scratch_shapes=[pltpu.VMEM((tm, d), BF16)],
        compiler_params=pltpu.CompilerParams(
            dimension_semantics=("parallel", "parallel"),
            vmem_limit_bytes=VMEM_LIMIT_BYTES),
        name="xattn",
    )(x, gain, wq, kv, wo)


def _layer(x, mem, ffn1_norm, ffn1_w_gate, ffn1_w_up, ffn1_w_down, mix_norm, w_in, pool_w,
           pool_scale, gla_w_a2, gla_b_a, gla_head_norm, w_out, xattn_norm, mem_norm,
           xattn_w_q, xattn_w_kv, xattn_w_o, ffn2_norm, ffn2_w_gate, ffn2_w_up, ffn2_w_down,
           final_gain, *, final_norm):
    b, s, d = x.shape
    row = lambda v: v.reshape(1, -1)
    bf = lambda w: w.astype(BF16)

    x = _ffn(x.reshape(b * s, d), row(ffn1_norm), bf(ffn1_w_gate), bf(ffn1_w_up), bf(ffn1_w_down),
             row(final_gain), final_norm=False).reshape(b, s, d)

    w_a = jnp.pad(w_in[:, MAIN_COLS:], ((0, 0), (0, LANES - GLA_GATE_RANK)))
    w_a2 = jnp.pad(gla_w_a2, ((0, LANES - GLA_GATE_RANK), (0, 0)))
    x = _mixer(x, row(mix_norm), bf(w_in[:, :MAIN_COLS]), bf(w_a), bf(w_a2), row(gla_b_a),
               bf(pool_w), row(pool_scale), row(gla_head_norm), bf(w_out))

    mb, ml, _ = mem.shape
    kv = _mem_kv(mem.reshape(mb * ml, d), row(mem_norm), bf(xattn_w_kv)).reshape(mb, ml, 2 * d)
    x = _xattn(x, row(xattn_norm), bf(xattn_w_q), kv, bf(xattn_w_o))

    x = _ffn(x.reshape(b * s, d), row(ffn2_norm), bf(ffn2_w_gate), bf(ffn2_w_up), bf(ffn2_w_down),
             row(final_gain), final_norm=final_norm).reshape(b, s, d)
    return x


def kernel(x, mem, ffn1_norm, ffn1_w_gate, ffn1_w_up, ffn1_w_down, mix_norm, w_in, pool_w, pool_scale, gla_w_a2, gla_b_a, gla_head_norm, w_out, xattn_norm, mem_norm, xattn_w_q, xattn_w_kv, xattn_w_o, ffn2_norm, ffn2_w_gate, ffn2_w_up, ffn2_w_down, final_norm):
    depth = ffn1_norm.shape[0]
    for l in range(depth):
        x = _layer(x, mem, ffn1_norm[l], ffn1_w_gate[l], ffn1_w_up[l], ffn1_w_down[l], mix_norm[l],
                   w_in[l], pool_w[l], pool_scale[l], gla_w_a2[l], gla_b_a[l], gla_head_norm[l],
                   w_out[l], xattn_norm[l], mem_norm[l], xattn_w_q[l], xattn_w_kv[l], xattn_w_o[l],
                   ffn2_norm[l], ffn2_w_gate[l], ffn2_w_up[l], ffn2_w_down[l], final_norm,
                   final_norm=(l == depth - 1))
    return x
```

```python
import functools

import jax
import jax.numpy as jnp
from jax import lax
from jax.experimental import pallas as pl
from jax.experimental.pallas import tpu as pltpu

F32 = jnp.float32
BF16 = jnp.bfloat16

D_MODEL = 2048
CHUNK = 64
D_POOL = 1024
POOL_WINDOWS = (2, 4, 8, 16)
POOL_GROUP = 256
POOL_HALO = 16
D_GLA = 1024
GLA_HEADS = 4
GLA_DV = 256
GLA_DK = 128
D_GLA_K = GLA_HEADS * GLA_DK
GLA_GATE_RANK = 16
GLA_GATE_TEMP = 16.0
XATTN_HEADS = 4
XATTN_HEAD_DIM = D_MODEL // XATTN_HEADS
RMS_EPS = 1e-6
LANES = 128

Q_OFF = D_POOL
K_OFF = Q_OFF + D_GLA_K
V_OFF = K_OFF + D_GLA_K
G_OFF = V_OFF + D_GLA
MAIN_COLS = G_OFF + D_GLA

VMEM_LIMIT_BYTES = 56 * 1024 * 1024

FFN_TM = 512
FFN_TF = 512
MIX_TM = 256
XATTN_TM = 256
KV_TN = 1024


def _rms(x, gain):
    return x * lax.rsqrt(jnp.mean(x * x, axis=-1, keepdims=True) + RMS_EPS) * gain


def _dot(a, b):
    return jnp.dot(a, b, preferred_element_type=F32)


def _resident(shape):
    return pl.BlockSpec(shape, lambda *_: (0,) * len(shape), pipeline_mode=pl.Buffered(1))


def _ffn_kernel(x_ref, gain_ref, wg_ref, wu_ref, wd_ref, fgain_ref, o_ref, h_ref, *, final_norm):
    j = pl.program_id(1)
    last = pl.num_programs(1) - 1

    @pl.when(j == 0)
    def _():
        h_ref[...] = _rms(x_ref[...], gain_ref[...]).astype(BF16)
        o_ref[...] = jnp.zeros_like(o_ref)

    h = h_ref[...]
    g = _dot(h, wg_ref[...])
    u = _dot(h, wu_ref[...])
    a = (g * jax.nn.sigmoid(g) * u).astype(BF16)
    o_ref[...] += _dot(a, wd_ref[...])

    @pl.when(j == last)
    def _():
        y = x_ref[...] + 0.5 * o_ref[...]
        if final_norm:
            y = _rms(y, fgain_ref[...])
        o_ref[...] = y


def _ffn(x, gain, wg, wu, wd, fgain, *, final_norm):
    t, d = x.shape
    dff = wg.shape[1]
    grid = (t // FFN_TM, dff // FFN_TF)
    return pl.pallas_call(
        functools.partial(_ffn_kernel, final_norm=final_norm),
        grid=grid,
        in_specs=[
            pl.BlockSpec((FFN_TM, d), lambda i, j: (i, 0)),
            pl.BlockSpec((1, d), lambda i, j: (0, 0)),
            pl.BlockSpec((d, FFN_TF), lambda i, j: (0, j)),
            pl.BlockSpec((d, FFN_TF), lambda i, j: (0, j)),
            pl.BlockSpec((FFN_TF, d), lambda i, j: (j, 0)),
            pl.BlockSpec((1, d), lambda i, j: (0, 0)),
        ],
        out_specs=pl.BlockSpec((FFN_TM, d), lambda i, j: (i, 0)),
        out_shape=jax.ShapeDtypeStruct((t, d), F32),
        scratch_shapes=[pltpu.VMEM((FFN_TM, d), BF16)],
        compiler_params=pltpu.CompilerParams(
            dimension_semantics=("parallel", "arbitrary"),
            vmem_limit_bytes=VMEM_LIMIT_BYTES),
        name="ffn_final" if final_norm else "ffn",
    )(x, gain, wg, wu, wd, fgain)


def _mix_kernel(x_ref, gain_ref, win_ref, wa_ref, wa2_ref, ba_ref, poolw_ref, pscale_ref,
                hnorm_ref, wout_ref, o_ref,
                proj_ref, oacc_ref, ubuf_ref, cat_ref, state_ref):
    tm = x_ref.shape[0]
    s = pl.program_id(1)

    @pl.when(s == 0)
    def _():
        state_ref[...] = jnp.zeros_like(state_ref)
        ubuf_ref[0:POOL_HALO, :] = jnp.zeros((POOL_HALO, D_POOL), F32)

    h = _rms(x_ref[...], gain_ref[...]).astype(BF16)
    proj_ref[...] = _dot(h, win_ref[...])
    a_lr = _dot(h, wa_ref[...])
    z = _dot(a_lr.astype(BF16), wa2_ref[...]) + ba_ref[...]
    la = (jnp.minimum(z, 0.0) - jnp.log1p(jnp.exp(-jnp.abs(z)))) * (1.0 / GLA_GATE_TEMP)

    ubuf_ref[POOL_HALO:POOL_HALO + tm, :] = proj_ref[:, 0:D_POOL]
    t_idx = s * tm + lax.broadcasted_iota(jnp.int32, (tm, 1), 0)
    count = (t_idx + 1).astype(F32)
    for gi, w in enumerate(POOL_WINDOWS):
        cols = slice(gi * POOL_GROUP, (gi + 1) * POOL_GROUP)
        ext = ubuf_ref[:, cols]
        acc = ext
        shift = 1
        while shift < w:
            acc = acc + pltpu.roll(acc, shift, 0)
            shift *= 2
        win = acc[POOL_HALO:, :]
        d = win / jnp.minimum(count, float(w)) - ext[POOL_HALO:, :]
        y = _dot(d.astype(BF16), poolw_ref[gi]) * pscale_ref[:, cols]
        cat_ref[:, cols] = y.astype(BF16)
    ubuf_ref[0:POOL_HALO, :] = ubuf_ref[tm:tm + POOL_HALO, :]

    row = lax.broadcasted_iota(jnp.int32, (tm, tm), 0)
    col = lax.broadcasted_iota(jnp.int32, (tm, tm), 1)
    same_chunk = (row // CHUNK) == (col // CHUNK)
    chunk_sum = same_chunk.astype(BF16)
    chunk_prefix = (same_chunk & (row >= col)).astype(BF16)
    la_hi = la.astype(BF16)
    la_lo = (la - la_hi.astype(F32)).astype(BF16)
    cum = _dot(chunk_prefix, la_hi) + _dot(chunk_prefix, la_lo)
    b_end = _dot(chunk_sum, la_hi) + _dot(chunk_sum, la_lo)
    k_dec = (proj_ref[:, K_OFF:K_OFF + D_GLA_K] * jnp.exp(b_end - cum)).astype(BF16)
    decay = jnp.exp(b_end)
    q = (proj_ref[:, Q_OFF:Q_OFF + D_GLA_K] * (GLA_DK ** -0.5)).astype(BF16)
    n_chunks = tm // CHUNK
    heads_chunks = [(hd, c) for hd in range(GLA_HEADS) for c in range(n_chunks)]
    upd = {}
    for hd, c in heads_chunks:
        rows = slice(c * CHUNK, (c + 1) * CHUNK)
        v_h = proj_ref[rows, V_OFF + hd * GLA_DV:V_OFF + (hd + 1) * GLA_DV].astype(BF16)
        upd[hd, c] = lax.dot_general(v_h, k_dec[rows, hd * GLA_DK:(hd + 1) * GLA_DK],
                                     (((0,), (0,)), ((), ())),
                                     preferred_element_type=F32)
    states = {}
    for hd in range(GLA_HEADS):
        st = state_ref[hd]
        for c in range(n_chunks):
            st = st * decay[c * CHUNK:c * CHUNK + 1, hd * GLA_DK:(hd + 1) * GLA_DK] + upd[hd, c]
            states[hd, c] = st.astype(BF16)
        state_ref[hd] = st
    for hd, c in heads_chunks:
        rows = slice(c * CHUNK, (c + 1) * CHUNK)
        oacc_ref[rows, hd * GLA_DV:(hd + 1) * GLA_DV] = lax.dot_general(
            q[rows, hd * GLA_DK:(hd + 1) * GLA_DK], states[hd, c], (((1,), (1,)), ((), ())),
            preferred_element_type=F32)
    for hd in range(GLA_HEADS):
        hc = slice(hd * GLA_DV, (hd + 1) * GLA_DV)
        o = oacc_ref[:, hc]
        o = o * lax.rsqrt(jnp.mean(o * o, axis=-1, keepdims=True) + RMS_EPS) * hnorm_ref[:, hc]
        g_h = proj_ref[:, G_OFF + hd * GLA_DV:G_OFF + (hd + 1) * GLA_DV]
        y = o * (g_h * jax.nn.sigmoid(g_h))
        cat_ref[:, D_POOL + hd * GLA_DV:D_POOL + (hd + 1) * GLA_DV] = y.astype(BF16)

    o_ref[...] = x_ref[...] + _dot(cat_ref[...], wout_ref[...])


def _mixer(x, gain, win, wa, wa2, ba, poolw, pscale, hnorm, wout):
    b, s, d = x.shape
    tm = MIX_TM
    return pl.pallas_call(
        _mix_kernel,
        grid=(b, s // tm),
        in_specs=[
            pl.BlockSpec((None, tm, d), lambda i, j: (i, j, 0)),
            _resident((1, d)),
            _resident((d, MAIN_COLS)),
            _resident((d, LANES)),
            _resident((LANES, D_GLA_K)),
            _resident((1, D_GLA_K)),
            _resident((len(POOL_WINDOWS), POOL_GROUP, POOL_GROUP)),
            _resident((1, D_POOL)),
            _resident((1, D_GLA)),
            _resident((d, d)),
        ],
        out_specs=pl.BlockSpec((None, tm, d), lambda i, j: (i, j, 0)),
        out_shape=jax.ShapeDtypeStruct((b, s, d), F32),
        scratch_shapes=[
            pltpu.VMEM((tm, MAIN_COLS), F32),
            pltpu.VMEM((tm, D_GLA), F32),
            pltpu.VMEM((POOL_HALO + tm, D_POOL), F32),
            pltpu.VMEM((tm, d), BF16),
            pltpu.VMEM((GLA_HEADS, GLA_DV, GLA_DK), F32),
        ],
        compiler_params=pltpu.CompilerParams(
            dimension_semantics=("arbitrary", "arbitrary"),
            vmem_limit_bytes=VMEM_LIMIT_BYTES),
        name="mixer",
    )(x, gain, win, wa, wa2, ba, poolw, pscale, hnorm, wout)


def _kv_kernel(mem_ref, gain_ref, wkv_ref, o_ref):
    h = _rms(mem_ref[...], gain_ref[...]).astype(BF16)
    o_ref[...] = _dot(h, wkv_ref[...]).astype(BF16)


def _mem_kv(mem, gain, wkv):
    t, d = mem.shape
    n = wkv.shape[1]
    return pl.pallas_call(
        _kv_kernel,
        grid=(n // KV_TN,),
        in_specs=[
            pl.BlockSpec((t, d), lambda j: (0, 0)),
            pl.BlockSpec((1, d), lambda j: (0, 0)),
            pl.BlockSpec((d, KV_TN), lambda j: (0, j)),
        ],
        out_specs=pl.BlockSpec((t, KV_TN), lambda j: (0, j)),
        out_shape=jax.ShapeDtypeStruct((t, n), BF16),
        compiler_params=pltpu.CompilerParams(
            dimension_semantics=("arbitrary",),
            vmem_limit_bytes=VMEM_LIMIT_BYTES),
        name="mem_kv",
    )(mem, gain, wkv)


def _xattn_kernel(x_ref, gain_ref, wq_ref, kv_ref, wo_ref, o_ref, cat_ref):
    h = _rms(x_ref[...], gain_ref[...]).astype(BF16)
    q = (_dot(h, wq_ref[...]) * (XATTN_HEAD_DIM ** -0.5)).astype(BF16)
    for hd in range(XATTN_HEADS):
        cols = slice(hd * XATTN_HEAD_DIM, (hd + 1) * XATTN_HEAD_DIM)
        k_h = kv_ref[:, cols]
        v_h = kv_ref[:, D_MODEL + hd * XATTN_HEAD_DIM:D_MODEL + (hd + 1) * XATTN_HEAD_DIM]
        logits = lax.dot_general(q[:, cols], k_h, (((1,), (1,)), ((), ())),
                                 preferred_element_type=F32)
        m = jnp.max(logits, axis=-1, keepdims=True)
        e = jnp.exp(logits - m)
        p = e / jnp.sum(e, axis=-1, keepdims=True)
        cat_ref[:, cols] = _dot(p.astype(BF16), v_h).astype(BF16)
    o_ref[...] = x_ref[...] + _dot(cat_ref[...], wo_ref[...])


def _xattn(x, gain, wq, kv, wo):
    b, s, d = x.shape
    m = kv.shape[1]
    tm = XATTN_TM
    return pl.pallas_call(
        _xattn_kernel,
        grid=(b, s // tm),
        in_specs=[
            pl.BlockSpec((None, tm, d), lambda i, j: (i, j, 0)),
            _resident((1, d)),
            _resident((d, d)),
            pl.BlockSpec((None, m, 2 * d), lambda i, j: (i, 0, 0)),
            _resident((d, d)),
        ],
        out_specs=pl.BlockSpec((None, tm, d), lambda i, j: (i, j, 0)),
        out_shape=jax.ShapeDtypeStruct((b, s, d), F32),
        scratch_shapes=[pltpu.VMEM((tm, d), BF16)],
        compiler_params=pltpu.CompilerParams(
            dimension_semantics=("parallel", "parallel"),
            vmem_limit_bytes=VMEM_LIMIT_BYTES),
        name="xattn",
    )(x, gain, wq, kv, wo)


def _layer(x, mem, ffn1_norm, ffn1_w_gate, ffn1_w_up, ffn1_w_down, mix_norm, w_in, pool_w,
           pool_scale, gla_w_a2, gla_b_a, gla_head_norm, w_out, xattn_norm, mem_norm,
           xattn_w_q, xattn_w_kv, xattn_w_o, ffn2_norm, ffn2_w_gate, ffn2_w_up, ffn2_w_down,
           final_gain, *, final_norm):
    b, s, d = x.shape
    row = lambda v: v.reshape(1, -1)
    bf = lambda w: w.astype(BF16)

    x = _ffn(x.reshape(b * s, d), row(ffn1_norm), bf(ffn1_w_gate), bf(ffn1_w_up), bf(ffn1_w_down),
             row(final_gain), final_norm=False).reshape(b, s, d)

    w_a = jnp.pad(w_in[:, MAIN_COLS:], ((0, 0), (0, LANES - GLA_GATE_RANK)))
    w_a2 = jnp.pad(gla_w_a2, ((0, LANES - GLA_GATE_RANK), (0, 0)))
    x = _mixer(x, row(mix_norm), bf(w_in[:, :MAIN_COLS]), bf(w_a), bf(w_a2), row(gla_b_a),
               bf(pool_w), row(pool_scale), row(gla_head_norm), bf(w_out))

    mb, ml, _ = mem.shape
    kv = _mem_kv(mem.reshape(mb * ml, d), row(mem_norm), bf(xattn_w_kv)).reshape(mb, ml, 2 * d)
    x = _xattn(x, row(xattn_norm), bf(xattn_w_q), kv, bf(xattn_w_o))

    x = _ffn(x.reshape(b * s, d), row(ffn2_norm), bf(ffn2_w_gate), bf(ffn2_w_up), bf(ffn2_w_down),
             row(final_gain), final_norm=final_norm).reshape(b, s, d)
    return x


def kernel(x, mem, ffn1_norm, ffn1_w_gate, ffn1_w_up, ffn1_w_down, mix_norm, w_in, pool_w, pool_scale, gla_w_a2, gla_b_a, gla_head_norm, w_out, xattn_norm, mem_norm, xattn_w_q, xattn_w_kv, xattn_w_o, ffn2_norm, ffn2_w_gate, ffn2_w_up, ffn2_w_down, final_norm):
    depth = ffn1_norm.shape[0]
    for l in range(depth):
        x = _layer(x, mem, ffn1_norm[l], ffn1_w_gate[l], ffn1_w_up[l], ffn1_w_down[l], mix_norm[l],
                   w_in[l], pool_w[l], pool_scale[l], gla_w_a2[l], gla_b_a[l], gla_head_norm[l],
                   w_out[l], xattn_norm[l], mem_norm[l], xattn_w_q[l], xattn_w_kv[l], xattn_w_o[l],
                   ffn2_norm[l], ffn2_w_gate[l], ffn2_w_up[l], ffn2_w_down[l], final_norm,
                   final_norm=(l == depth - 1))
    return x
```

```python
import functools

import jax
import jax.numpy as jnp
from jax import lax
from jax.experimental import pallas as pl
from jax.experimental.pallas import tpu as pltpu

F32 = jnp.float32
BF16 = jnp.bfloat16

D_MODEL = 2048
CHUNK = 64
D_POOL = 1024
POOL_WINDOWS = (2, 4, 8, 16)
POOL_GROUP = 256
POOL_HALO = 16
D_GLA = 1024
GLA_HEADS = 4
GLA_DV = 256
GLA_DK = 128
D_GLA_K = GLA_HEADS * GLA_DK
GLA_GATE_RANK = 16
GLA_GATE_TEMP = 16.0
XATTN_HEADS = 4
XATTN_HEAD_DIM = D_MODEL // XATTN_HEADS
RMS_EPS = 1e-6
LANES = 128
BF16_SUBLANES = 16

Q_OFF = D_POOL
K_OFF = Q_OFF + D_GLA_K
V_OFF = K_OFF + D_GLA_K
G_OFF = V_OFF + D_GLA
MAIN_COLS = G_OFF + D_GLA

VMEM_LIMIT_BYTES = 62 * 1024 * 1024

FFN_TM = 1024
FFN_TF = 512
FFN_ROW_CHUNK = 256
MIX_TM = 256
XATTN_TM = 512
KV_TN = 1024


def _rms(x, gain):
    return x * lax.rsqrt(jnp.mean(x * x, axis=-1, keepdims=True) + RMS_EPS) * gain


def _dot(a, b):
    return jnp.dot(a, b, preferred_element_type=F32)


def _resident(shape):
    return pl.BlockSpec(shape, lambda *_: (0,) * len(shape), pipeline_mode=pl.Buffered(1))


def _ffn_kernel(x_ref, gain_ref, wg_ref, wu_ref, wd_ref, fgain_ref, *rest, final_norm, n_side):
    side_in, (o_ref, *side_out), h_ref = rest[:n_side], rest[n_side:2 * n_side + 1], rest[-1]
    j = pl.program_id(1)
    last = pl.num_programs(1) - 1

    n_row_chunks = x_ref.shape[0] // FFN_ROW_CHUNK

    def chunk_rows(c):
        return pl.ds(pl.multiple_of(c * FFN_ROW_CHUNK, FFN_ROW_CHUNK), FFN_ROW_CHUNK)

    @pl.when(j == 0)
    def _():
        def body(c, carry):
            rows = chunk_rows(c)
            h_ref[rows, :] = _rms(x_ref[rows, :], gain_ref[...]).astype(BF16)
            o_ref[rows, :] = jnp.zeros((FFN_ROW_CHUNK, o_ref.shape[1]), F32)
            return carry
        lax.fori_loop(0, n_row_chunks, body, 0)

    h = h_ref[...]
    g = _dot(h, wg_ref[...])
    for src_ref, dst_ref in zip(side_in, side_out):
        dst_ref[...] = src_ref[...].astype(BF16)
    u = _dot(h, wu_ref[...])
    a = (g * jax.nn.sigmoid(g) * u).astype(BF16)
    o_ref[...] += _dot(a, wd_ref[...])

    @pl.when(j == last)
    def _():
        def body(c, carry):
            rows = chunk_rows(c)
            y = x_ref[rows, :] + 0.5 * o_ref[rows, :]
            if final_norm:
                y = _rms(y, fgain_ref[...])
            o_ref[rows, :] = y
            return carry
        lax.fori_loop(0, n_row_chunks, body, 0)


def _side_cast_block_rows(n_rows, n_steps):
    rows = BF16_SUBLANES
    while n_rows % rows or n_rows // rows > n_steps:
        rows += BF16_SUBLANES
    return rows


def _ffn(x, gain, wg, wu, wd, fgain, *, final_norm, side_casts=()):
    t, d = x.shape
    dff = wg.shape[1]
    grid = (t // FFN_TM, dff // FFN_TF)
    n_steps = grid[0] * grid[1]
    side_specs_in, side_specs_out, side_shapes = [], [], []
    for w, n_cols in side_casts:
        rows = _side_cast_block_rows(w.shape[0], n_steps)
        n_blocks = w.shape[0] // rows
        index_map = functools.partial(
            lambda i, j, n_blocks: (jnp.minimum(i * grid[1] + j, n_blocks - 1), 0), n_blocks=n_blocks)
        side_specs_in.append(pl.BlockSpec((rows, n_cols), index_map))
        side_specs_out.append(pl.BlockSpec((rows, n_cols), index_map))
        side_shapes.append(jax.ShapeDtypeStruct((w.shape[0], n_cols), BF16))
    outs = pl.pallas_call(
        functools.partial(_ffn_kernel, final_norm=final_norm, n_side=len(side_casts)),
        grid=grid,
        in_specs=[
            pl.BlockSpec((FFN_TM, d), lambda i, j: (i, 0)),
            pl.BlockSpec((1, d), lambda i, j: (0, 0)),
            pl.BlockSpec((d, FFN_TF), lambda i, j: (0, j)),
            pl.BlockSpec((d, FFN_TF), lambda i, j: (0, j)),
            pl.BlockSpec((FFN_TF, d), lambda i, j: (j, 0)),
            pl.BlockSpec((1, d), lambda i, j: (0, 0)),
        ] + side_specs_in,
        out_specs=[pl.BlockSpec((FFN_TM, d), lambda i, j: (i, 0))] + side_specs_out,
        out_shape=[jax.ShapeDtypeStruct((t, d), F32)] + side_shapes,
        scratch_shapes=[pltpu.VMEM((FFN_TM, d), BF16)],
        compiler_params=pltpu.CompilerParams(
            dimension_semantics=("arbitrary", "arbitrary"),
            vmem_limit_bytes=VMEM_LIMIT_BYTES),
        name="ffn_final" if final_norm else "ffn",
    )(x, gain, wg, wu, wd, fgain, *[w for w, _ in side_casts])
    return outs[0], tuple(outs[1:])


def _mix_kernel(x_ref, gain_ref, win_ref, wa_ref, wa2_ref, ba_ref, poolw_ref, pscale_ref,
                hnorm_ref, wout_ref, o_ref,
                proj_ref, oacc_ref, ubuf_ref, cat_ref, state_ref):
    tm = x_ref.shape[0]
    s = pl.program_id(1)

    @pl.when(s == 0)
    def _():
        state_ref[...] = jnp.zeros_like(state_ref)
        ubuf_ref[0:POOL_HALO, :] = jnp.zeros((POOL_HALO, D_POOL), F32)

    h = _rms(x_ref[...], gain_ref[...]).astype(BF16)
    proj_ref[...] = _dot(h, win_ref[...])
    a_lr = _dot(h, wa_ref[...])
    z = _dot(a_lr.astype(BF16), wa2_ref[...]) + ba_ref[...]
    la = (jnp.minimum(z, 0.0) - jnp.log1p(jnp.exp(-jnp.abs(z)))) * (1.0 / GLA_GATE_TEMP)

    ubuf_ref[POOL_HALO:POOL_HALO + tm, :] = proj_ref[:, 0:D_POOL]
    t_idx = s * tm + lax.broadcasted_iota(jnp.int32, (tm, 1), 0)
    count = (t_idx + 1).astype(F32)
    for gi, w in enumerate(POOL_WINDOWS):
        cols = slice(gi * POOL_GROUP, (gi + 1) * POOL_GROUP)
        ext = ubuf_ref[:, cols]
        acc = ext
        shift = 1
        while shift < w:
            acc = acc + pltpu.roll(acc, shift, 0)
            shift *= 2
        win = acc[POOL_HALO:, :]
        d = win / jnp.minimum(count, float(w)) - ext[POOL_HALO:, :]
        y = _dot(d.astype(BF16), poolw_ref[gi]) * pscale_ref[:, cols]
        cat_ref[:, cols] = y.astype(BF16)
    ubuf_ref[0:POOL_HALO, :] = ubuf_ref[tm:tm + POOL_HALO, :]

    row = lax.broadcasted_iota(jnp.int32, (tm, tm), 0)
    col = lax.broadcasted_iota(jnp.int32, (tm, tm), 1)
    same_chunk = (row // CHUNK) == (col // CHUNK)
    chunk_sum = same_chunk.astype(BF16)
    chunk_prefix = (same_chunk & (row >= col)).astype(BF16)
    la_hi = la.astype(BF16)
    la_lo = (la - la_hi.astype(F32)).astype(BF16)
    cum = _dot(chunk_prefix, la_hi) + _dot(chunk_prefix, la_lo)
    b_end = _dot(chunk_sum, la_hi) + _dot(chunk_sum, la_lo)
    k_dec = (proj_ref[:, K_OFF:K_OFF + D_GLA_K] * jnp.exp(b_end - cum)).astype(BF16)
    decay = jnp.exp(b_end)
    q = (proj_ref[:, Q_OFF:Q_OFF + D_GLA_K] * (GLA_DK ** -0.5)).astype(BF16)
    n_chunks = tm // CHUNK
    heads_chunks = [(hd, c) for hd in range(GLA_HEADS) for c in range(n_chunks)]
    upd = {}
    for hd, c in heads_chunks:
        rows = slice(c * CHUNK, (c + 1) * CHUNK)
        v_h = proj_ref[rows, V_OFF + hd * GLA_DV:V_OFF + (hd + 1) * GLA_DV].astype(BF16)
        upd[hd, c] = lax.dot_general(v_h, k_dec[rows, hd * GLA_DK:(hd + 1) * GLA_DK],
                                     (((0,), (0,)), ((), ())),
                                     preferred_element_type=F32)
    states = {}
    for hd in range(GLA_HEADS):
        st = state_ref[hd]
        for c in range(n_chunks):
            st = st * decay[c * CHUNK:c * CHUNK + 1, hd * GLA_DK:(hd + 1) * GLA_DK] + upd[hd, c]
            states[hd, c] = st.astype(BF16)
        state_ref[hd] = st
    for hd, c in heads_chunks:
        rows = slice(c * CHUNK, (c + 1) * CHUNK)
        oacc_ref[rows, hd * GLA_DV:(hd + 1) * GLA_DV] = lax.dot_general(
            q[rows, hd * GLA_DK:(hd + 1) * GLA_DK], states[hd, c], (((1,), (1,)), ((), ())),
            preferred_element_type=F32)
    for hd in range(GLA_HEADS):
        hc = slice(hd * GLA_DV, (hd + 1) * GLA_DV)
        o = oacc_ref[:, hc]
        o = o * lax.rsqrt(jnp.mean(o * o, axis=-1, keepdims=True) + RMS_EPS) * hnorm_ref[:, hc]
        g_h = proj_ref[:, G_OFF + hd * GLA_DV:G_OFF + (hd + 1) * GLA_DV]
        y = o * (g_h * jax.nn.sigmoid(g_h))
        cat_ref[:, D_POOL + hd * GLA_DV:D_POOL + (hd + 1) * GLA_DV] = y.astype(BF16)

    o_ref[...] = x_ref[...] + _dot(cat_ref[...], wout_ref[...])


def _mixer(x, gain, win, wa, wa2, ba, poolw, pscale, hnorm, wout):
    b, s, d = x.shape
    tm = MIX_TM
    return pl.pallas_call(
        _mix_kernel,
        grid=(b, s // tm),
        in_specs=[
            pl.BlockSpec((None, tm, d), lambda i, j: (i, j, 0)),
            _resident((1, d)),
            _resident((d, MAIN_COLS)),
            _resident((d, LANES)),
            _resident((LANES, D_GLA_K)),
            _resident((1, D_GLA_K)),
            _resident((len(POOL_WINDOWS), POOL_GROUP, POOL_GROUP)),
            _resident((1, D_POOL)),
            _resident((1, D_GLA)),
            _resident((d, d)),
        ],
        out_specs=pl.BlockSpec((None, tm, d), lambda i, j: (i, j, 0)),
        out_shape=jax.ShapeDtypeStruct((b, s, d), F32),
        scratch_shapes=[
            pltpu.VMEM((tm, MAIN_COLS), F32),
            pltpu.VMEM((tm, D_GLA), F32),
            pltpu.VMEM((POOL_HALO + tm, D_POOL), F32),
            pltpu.VMEM((tm, d), BF16),
            pltpu.VMEM((GLA_HEADS, GLA_DV, GLA_DK), F32),
        ],
        compiler_params=pltpu.CompilerParams(
            dimension_semantics=("arbitrary", "arbitrary"),
            vmem_limit_bytes=VMEM_LIMIT_BYTES),
        name="mixer",
    )(x, gain, win, wa, wa2, ba, poolw, pscale, hnorm, wout)


def _kv_kernel(mem_ref, gain_ref, wkv_ref, o_ref):
    h = _rms(mem_ref[...], gain_ref[...]).astype(BF16)
    o_ref[...] = _dot(h, wkv_ref[...]).astype(BF16)


def _mem_kv(mem, gain, wkv):
    t, d = mem.shape
    n = wkv.shape[1]
    return pl.pallas_call(
        _kv_kernel,
        grid=(n // KV_TN,),
        in_specs=[
            pl.BlockSpec((t, d), lambda j: (0, 0)),
            pl.BlockSpec((1, d), lambda j: (0, 0)),
            pl.BlockSpec((d, KV_TN), lambda j: (0, j)),
        ],
        out_specs=pl.BlockSpec((t, KV_TN), lambda j: (0, j)),
        out_shape=jax.ShapeDtypeStruct((t, n), BF16),
        compiler_params=pltpu.CompilerParams(
            dimension_semantics=("arbitrary",),
            vmem_limit_bytes=VMEM_LIMIT_BYTES),
        name="mem_kv",
    )(mem, gain, wkv)


def _xattn_kernel(x_ref, gain_ref, wq_ref, kv_ref, wo_ref, o_ref, cat_ref):
    h = _rms(x_ref[...], gain_ref[...]).astype(BF16)
    q = (_dot(h, wq_ref[...]) * (XATTN_HEAD_DIM ** -0.5)).astype(BF16)
    for hd in range(XATTN_HEADS):
        cols = slice(hd * XATTN_HEAD_DIM, (hd + 1) * XATTN_HEAD_DIM)
        k_h = kv_ref[:, cols]
        v_h = kv_ref[:, D_MODEL + hd * XATTN_HEAD_DIM:D_MODEL + (hd + 1) * XATTN_HEAD_DIM]
        logits = lax.dot_general(q[:, cols], k_h, (((1,), (1,)), ((), ())),
                                 preferred_element_type=F32)
        m = jnp.max(logits, axis=-1, keepdims=True)
        e = jnp.exp(logits - m)
        p = e / jnp.sum(e, axis=-1, keepdims=True)
        cat_ref[:, cols] = _dot(p.astype(BF16), v_h).astype(BF16)
    o_ref[...] = x_ref[...] + _dot(cat_ref[...], wo_ref[...])


def _xattn(x, gain, wq, kv, wo):
    b, s, d = x.shape
    m = kv.shape[1]
    tm = XATTN_TM
    return pl.pallas_call(
        _xattn_kernel,
        grid=(b, s // tm),
        in_specs=[
            pl.BlockSpec((None, tm, d), lambda i, j: (i, j, 0)),
            _resident((1, d)),
            _resident((d, d)),
            pl.BlockSpec((None, m, 2 * d), lambda i, j: (i, 0, 0)),
            _resident((d, d)),
        ],
        out_specs=pl.BlockSpec((None, tm, d), lambda i, j: (i, j, 0)),
        out_shape=jax.ShapeDtypeStruct((b, s, d), F32),
        scratch_shapes=[pltpu.VMEM((tm, d), BF16)],
        compiler_params=pltpu.CompilerParams(
            dimension_semantics=("parallel", "parallel"),
            vmem_limit_bytes=VMEM_LIMIT_BYTES),
        name="xattn",
    )(x, gain, wq, kv, wo)


def _layer(x, mem, ffn1_norm, ffn1_w_gate, ffn1_w_up, ffn1_w_down, mix_norm, w_in, pool_w,
           pool_scale, gla_w_a2, gla_b_a, gla_head_norm, w_out, xattn_norm, mem_norm,
           xattn_w_q, xattn_w_kv, xattn_w_o, ffn2_norm, ffn2_w_gate, ffn2_w_up, ffn2_w_down,
           final_gain, *, final_norm):
    b, s, d = x.shape
    row = lambda v: v.reshape(1, -1)
    bf = lambda w: w.astype(BF16)

    later_weights = (ffn2_w_gate, ffn2_w_up, ffn2_w_down, w_in, w_out, xattn_w_q, xattn_w_kv, xattn_w_o)
    side_casts = tuple((w, MAIN_COLS if w is w_in else w.shape[1]) for w in later_weights)
    x, (w2g, w2u, w2d, w_main, w_o_mix, w_q, w_kv, w_o_att) = _ffn(
        x.reshape(b * s, d), row(ffn1_norm), bf(ffn1_w_gate), bf(ffn1_w_up), bf(ffn1_w_down),
        row(final_gain), final_norm=False, side_casts=side_casts)

    w_a = jnp.pad(w_in[:, MAIN_COLS:], ((0, 0), (0, LANES - GLA_GATE_RANK)))
    w_a2 = jnp.pad(gla_w_a2, ((0, LANES - GLA_GATE_RANK), (0, 0)))
    x = _mixer(x.reshape(b, s, d), row(mix_norm), w_main, bf(w_a), bf(w_a2), row(gla_b_a),
               bf(pool_w), row(pool_scale), row(gla_head_norm), w_o_mix)

    mb, ml, _ = mem.shape
    kv = _mem_kv(mem.reshape(mb * ml, d), row(mem_norm), w_kv).reshape(mb, ml, 2 * d)
    x = _xattn(x, row(xattn_norm), w_q, kv, w_o_att)

    x, _ = _ffn(x.reshape(b * s, d), row(ffn2_norm), w2g, w2u, w2d,
                row(final_gain), final_norm=final_norm)
    return x.reshape(b, s, d)


def kernel(x, mem, ffn1_norm, ffn1_w_gate, ffn1_w_up, ffn1_w_down, mix_norm, w_in, pool_w, pool_scale, gla_w_a2, gla_b_a, gla_head_norm, w_out, xattn_norm, mem_norm, xattn_w_q, xattn_w_kv, xattn_w_o, ffn2_norm, ffn2_w_gate, ffn2_w_up, ffn2_w_down, final_norm):
    depth = ffn1_norm.shape[0]
    for l in range(depth):
        x = _layer(x, mem, ffn1_norm[l], ffn1_w_gate[l], ffn1_w_up[l], ffn1_w_down[l], mix_norm[l],
                   w_in[l], pool_w[l], pool_scale[l], gla_w_a2[l], gla_b_a[l], gla_head_norm[l],
                   w_out[l], xattn_norm[l], mem_norm[l], xattn_w_q[l], xattn_w_kv[l], xattn_w_o[l],
                   ffn2_norm[l], ffn2_w_gate[l], ffn2_w_up[l], ffn2_w_down[l], final_norm,
                   final_norm=(l == depth - 1))
    return x
```

```python
import functools

import jax
import jax.numpy as jnp
from jax import lax
from jax.experimental import pallas as pl
from jax.experimental.pallas import tpu as pltpu

F32 = jnp.float32
BF16 = jnp.bfloat16

D_MODEL = 2048
CHUNK = 64
D_POOL = 1024
POOL_WINDOWS = (2, 4, 8, 16)
POOL_GROUP = 256
POOL_HALO = 16
D_GLA = 1024
GLA_HEADS = 4
GLA_DV = 256
GLA_DK = 128
D_GLA_K = GLA_HEADS * GLA_DK
GLA_GATE_RANK = 16
GLA_GATE_TEMP = 16.0
XATTN_HEADS = 4
XATTN_HEAD_DIM = D_MODEL // XATTN_HEADS
RMS_EPS = 1e-6
LANES = 128
BF16_SUBLANES = 16

Q_OFF = D_POOL
K_OFF = Q_OFF + D_GLA_K
V_OFF = K_OFF + D_GLA_K
G_OFF = V_OFF + D_GLA
MAIN_COLS = G_OFF + D_GLA

VMEM_LIMIT_BYTES = 62 * 1024 * 1024

FFN_TM = 1024
FFN_TF = 512
FFN_ROW_CHUNK = 256
FFN_HEAD_TF = 256
MIX_TM = 256
XATTN_TM = 512
KV_TN = 1024


def _rms(x, gain):
    return x * lax.rsqrt(jnp.mean(x * x, axis=-1, keepdims=True) + RMS_EPS) * gain


def _dot(a, b):
    return jnp.dot(a, b, preferred_element_type=F32)


def _resident(shape):
    return pl.BlockSpec(shape, lambda *_: (0,) * len(shape), pipeline_mode=pl.Buffered(1))


def _ffn_kernel(x_ref, gain_ref, wg_ref, wu_ref, wd_ref, fgain_ref, *rest, final_norm, n_side,
                emit_weights):
    n_w = 3 if emit_weights else 0
    side_in, o_ref, h_ref = rest[:n_side], rest[n_side], rest[-1]
    w_out = rest[n_side + 1:n_side + 1 + n_w]
    side_out = rest[n_side + 1 + n_w:2 * n_side + 1 + n_w]

    def weight(ref, k):
        if not emit_weights:
            return ref[...]
        w = ref[...].astype(BF16)
        w_out[k][...] = w
        return w

    j = pl.program_id(1)
    last = pl.num_programs(1) - 1

    n_row_chunks = x_ref.shape[0] // FFN_ROW_CHUNK

    def chunk_rows(c):
        return pl.ds(pl.multiple_of(c * FFN_ROW_CHUNK, FFN_ROW_CHUNK), FFN_ROW_CHUNK)

    @pl.when(j == 0)
    def _():
        def body(c, carry):
            rows = chunk_rows(c)
            h_ref[rows, :] = _rms(x_ref[rows, :], gain_ref[...]).astype(BF16)
            o_ref[rows, :] = jnp.zeros((FFN_ROW_CHUNK, o_ref.shape[1]), F32)
            return carry
        lax.fori_loop(0, n_row_chunks, body, 0)

    h = h_ref[...]
    g = _dot(h, weight(wg_ref, 0))
    for src_ref, dst_ref in zip(side_in, side_out):
        dst_ref[...] = src_ref[...].astype(BF16)
    u = _dot(h, weight(wu_ref, 1))
    a = (g * jax.nn.sigmoid(g) * u).astype(BF16)
    o_ref[...] += _dot(a, weight(wd_ref, 2))

    @pl.when(j == last)
    def _():
        def body(c, carry):
            rows = chunk_rows(c)
            y = x_ref[rows, :] + 0.5 * o_ref[rows, :]
            if final_norm:
                y = _rms(y, fgain_ref[...])
            o_ref[rows, :] = y
            return carry
        lax.fori_loop(0, n_row_chunks, body, 0)


def _side_cast_block_rows(n_rows, n_steps):
    rows = BF16_SUBLANES
    while n_rows % rows or n_rows // rows > n_steps:
        rows += BF16_SUBLANES
    return rows


def _ffn(x, gain, wg, wu, wd, fgain, *, final_norm, name, tiles=None, tf=FFN_TF, side_casts=()):
    d = x.shape[1]
    dff = wg.shape[1]
    first, count = tiles if tiles is not None else (0, x.shape[0] // FFN_TM)
    t = count * FFN_TM
    emit_weights = wg.dtype == F32
    grid = (count, dff // tf)
    n_steps = grid[0] * grid[1]
    w_specs = [
        pl.BlockSpec((d, tf), lambda i, j: (0, j)),
        pl.BlockSpec((d, tf), lambda i, j: (0, j)),
        pl.BlockSpec((tf, d), lambda i, j: (j, 0)),
    ]
    w_out_specs = w_specs if emit_weights else []
    w_out_shapes = [jax.ShapeDtypeStruct(w.shape, BF16) for w in (wg, wu, wd)] if emit_weights else []
    side_specs_in, side_specs_out, side_shapes = [], [], []
    for w, n_cols in side_casts:
        rows = _side_cast_block_rows(w.shape[0], n_steps)
        n_blocks = w.shape[0] // rows
        index_map = functools.partial(
            lambda i, j, n_blocks: (jnp.minimum(i * grid[1] + j, n_blocks - 1), 0), n_blocks=n_blocks)
        side_specs_in.append(pl.BlockSpec((rows, n_cols), index_map))
        side_specs_out.append(pl.BlockSpec((rows, n_cols), index_map))
        side_shapes.append(jax.ShapeDtypeStruct((w.shape[0], n_cols), BF16))
    outs = pl.pallas_call(
        functools.partial(_ffn_kernel, final_norm=final_norm, n_side=len(side_casts),
                          emit_weights=emit_weights),
        grid=grid,
        in_specs=[
            pl.BlockSpec((FFN_TM, d), lambda i, j: (first + i, 0)),
            pl.BlockSpec((1, d), lambda i, j: (0, 0)),
            *w_specs,
            pl.BlockSpec((1, d), lambda i, j: (0, 0)),
        ] + side_specs_in,
        out_specs=[pl.BlockSpec((FFN_TM, d), lambda i, j: (i, 0))] + w_out_specs + side_specs_out,
        out_shape=[jax.ShapeDtypeStruct((t, d), F32)] + w_out_shapes + side_shapes,
        scratch_shapes=[pltpu.VMEM((FFN_TM, d), BF16)],
        compiler_params=pltpu.CompilerParams(
            dimension_semantics=("arbitrary", "arbitrary"),
            vmem_limit_bytes=VMEM_LIMIT_BYTES),
        name=name,
    )(x, gain, wg, wu, wd, fgain, *[w for w, _ in side_casts])
    n_w = len(w_out_shapes)
    return outs[0], tuple(outs[1:1 + n_w]), tuple(outs[1 + n_w:])


def _mix_kernel(xh_ref, xt_ref, gain_ref, win_ref, wa_ref, wa2_ref, ba_ref, poolw_ref, pscale_ref,
                hnorm_ref, wout_ref, o_ref,
                proj_ref, oacc_ref, ubuf_ref, cat_ref, state_ref, *, n_head_tiles):
    tm = o_ref.shape[0]
    s = pl.program_id(1)
    in_head = pl.program_id(0) * pl.num_programs(1) + s < n_head_tiles

    def x_tile():
        return jnp.where(in_head, xh_ref[...], xt_ref[...])

    @pl.when(s == 0)
    def _():
        state_ref[...] = jnp.zeros_like(state_ref)
        ubuf_ref[0:POOL_HALO, :] = jnp.zeros((POOL_HALO, D_POOL), F32)

    h = _rms(x_tile(), gain_ref[...]).astype(BF16)
    proj_ref[...] = _dot(h, win_ref[...])
    a_lr = _dot(h, wa_ref[...])
    z = _dot(a_lr.astype(BF16), wa2_ref[...]) + ba_ref[...]
    la = (jnp.minimum(z, 0.0) - jnp.log1p(jnp.exp(-jnp.abs(z)))) * (1.0 / GLA_GATE_TEMP)

    ubuf_ref[POOL_HALO:POOL_HALO + tm, :] = proj_ref[:, 0:D_POOL]
    t_idx = s * tm + lax.broadcasted_iota(jnp.int32, (tm, 1), 0)
    count = (t_idx + 1).astype(F32)
    for gi, w in enumerate(POOL_WINDOWS):
        cols = slice(gi * POOL_GROUP, (gi + 1) * POOL_GROUP)
        ext = ubuf_ref[:, cols]
        acc = ext
        shift = 1
        while shift < w:
            acc = acc + pltpu.roll(acc, shift, 0)
            shift *= 2
        win = acc[POOL_HALO:, :]
        d = win / jnp.minimum(count, float(w)) - ext[POOL_HALO:, :]
        y = _dot(d.astype(BF16), poolw_ref[gi]) * pscale_ref[:, cols]
        cat_ref[:, cols] = y.astype(BF16)
    ubuf_ref[0:POOL_HALO, :] = ubuf_ref[tm:tm + POOL_HALO, :]

    row = lax.broadcasted_iota(jnp.int32, (tm, tm), 0)
    col = lax.broadcasted_iota(jnp.int32, (tm, tm), 1)
    same_chunk = (row // CHUNK) == (col // CHUNK)
    chunk_sum = same_chunk.astype(BF16)
    chunk_prefix = (same_chunk & (row >= col)).astype(BF16)
    la_hi = la.astype(BF16)
    la_lo = (la - la_hi.astype(F32)).astype(BF16)
    cum = _dot(chunk_prefix, la_hi) + _dot(chunk_prefix, la_lo)
    b_end = _dot(chunk_sum, la_hi) + _dot(chunk_sum, la_lo)
    k_dec = (proj_ref[:, K_OFF:K_OFF + D_GLA_K] * jnp.exp(b_end - cum)).astype(BF16)
    decay = jnp.exp(b_end)
    q = (proj_ref[:, Q_OFF:Q_OFF + D_GLA_K] * (GLA_DK ** -0.5)).astype(BF16)
    n_chunks = tm // CHUNK
    heads_chunks = [(hd, c) for hd in range(GLA_HEADS) for c in range(n_chunks)]
    upd = {}
    for hd, c in heads_chunks:
        rows = slice(c * CHUNK, (c + 1) * CHUNK)
        v_h = proj_ref[rows, V_OFF + hd * GLA_DV:V_OFF + (hd + 1) * GLA_DV].astype(BF16)
        upd[hd, c] = lax.dot_general(v_h, k_dec[rows, hd * GLA_DK:(hd + 1) * GLA_DK],
                                     (((0,), (0,)), ((), ())),
                                     preferred_element_type=F32)
    states = {}
    for hd in range(GLA_HEADS):
        st = state_ref[hd]
        for c in range(n_chunks):
            st = st * decay[c * CHUNK:c * CHUNK + 1, hd * GLA_DK:(hd + 1) * GLA_DK] + upd[hd, c]
            states[hd, c] = st.astype(BF16)
        state_ref[hd] = st
    for hd, c in heads_chunks:
        rows = slice(c * CHUNK, (c + 1) * CHUNK)
        oacc_ref[rows, hd * GLA_DV:(hd + 1) * GLA_DV] = lax.dot_general(
            q[rows, hd * GLA_DK:(hd + 1) * GLA_DK], states[hd, c], (((1,), (1,)), ((), ())),
            preferred_element_type=F32)
    for hd in range(GLA_HEADS):
        hc = slice(hd * GLA_DV, (hd + 1) * GLA_DV)
        o = oacc_ref[:, hc]
        o = o * lax.rsqrt(jnp.mean(o * o, axis=-1, keepdims=True) + RMS_EPS) * hnorm_ref[:, hc]
        g_h = proj_ref[:, G_OFF + hd * GLA_DV:G_OFF + (hd + 1) * GLA_DV]
        y = o * (g_h * jax.nn.sigmoid(g_h))
        cat_ref[:, D_POOL + hd * GLA_DV:D_POOL + (hd + 1) * GLA_DV] = y.astype(BF16)

    o_ref[...] = x_tile() + _dot(cat_ref[...], wout_ref[...])


def _mixer(x_head, x_tail, b, gain, win, wa, wa2, ba, poolw, pscale, hnorm, wout):
    d = x_head.shape[1]
    tm = MIX_TM
    n_head = x_head.shape[0] // tm
    s = (x_head.shape[0] + x_tail.shape[0]) // b
    n_s = s // tm
    return pl.pallas_call(
        functools.partial(_mix_kernel, n_head_tiles=n_head),
        grid=(b, n_s),
        in_specs=[
            pl.BlockSpec((tm, d), lambda i, j: (jnp.minimum(i * n_s + j, n_head - 1), 0)),
            pl.BlockSpec((tm, d), lambda i, j: (jnp.maximum(i * n_s + j - n_head, 0), 0)),
            _resident((1, d)),
            _resident((d, MAIN_COLS)),
            _resident((d, LANES)),
            _resident((LANES, D_GLA_K)),
            _resident((1, D_GLA_K)),
            _resident((len(POOL_WINDOWS), POOL_GROUP, POOL_GROUP)),
            _resident((1, D_POOL)),
            _resident((1, D_GLA)),
            _resident((d, d)),
        ],
        out_specs=pl.BlockSpec((None, tm, d), lambda i, j: (i, j, 0)),
        out_shape=jax.ShapeDtypeStruct((b, s, d), F32),
        scratch_shapes=[
            pltpu.VMEM((tm, MAIN_COLS), F32),
            pltpu.VMEM((tm, D_GLA), F32),
            pltpu.VMEM((POOL_HALO + tm, D_POOL), F32),
            pltpu.VMEM((tm, d), BF16),
            pltpu.VMEM((GLA_HEADS, GLA_DV, GLA_DK), F32),
        ],
        compiler_params=pltpu.CompilerParams(
            dimension_semantics=("arbitrary", "arbitrary"),
            vmem_limit_bytes=VMEM_LIMIT_BYTES),
        name="mixer",
    )(x_head, x_tail, gain, win, wa, wa2, ba, poolw, pscale, hnorm, wout)


def _kv_kernel(mem_ref, gain_ref, wkv_ref, o_ref):
    h = _rms(mem_ref[...], gain_ref[...]).astype(BF16)
    o_ref[...] = _dot(h, wkv_ref[...]).astype(BF16)


def _mem_kv(mem, gain, wkv):
    t, d = mem.shape
    n = wkv.shape[1]
    return pl.pallas_call(
        _kv_kernel,
        grid=(n // KV_TN,),
        in_specs=[
            pl.BlockSpec((t, d), lambda j: (0, 0)),
            pl.BlockSpec((1, d), lambda j: (0, 0)),
            pl.BlockSpec((d, KV_TN), lambda j: (0, j)),
        ],
        out_specs=pl.BlockSpec((t, KV_TN), lambda j: (0, j)),
        out_shape=jax.ShapeDtypeStruct((t, n), BF16),
        compiler_params=pltpu.CompilerParams(
            dimension_semantics=("arbitrary",),
            vmem_limit_bytes=VMEM_LIMIT_BYTES),
        name="mem_kv",
    )(mem, gain, wkv)


def _xattn_kernel(x_ref, gain_ref, wq_ref, kv_ref, wo_ref, o_ref, cat_ref):
    h = _rms(x_ref[...], gain_ref[...]).astype(BF16)
    q = (_dot(h, wq_ref[...]) * (XATTN_HEAD_DIM ** -0.5)).astype(BF16)
    for hd in range(XATTN_HEADS):
        cols = slice(hd * XATTN_HEAD_DIM, (hd + 1) * XATTN_HEAD_DIM)
        k_h = kv_ref[:, cols]
        v_h = kv_ref[:, D_MODEL + hd * XATTN_HEAD_DIM:D_MODEL + (hd + 1) * XATTN_HEAD_DIM]
        logits = lax.dot_general(q[:, cols], k_h, (((1,), (1,)), ((), ())),
                                 preferred_element_type=F32)
        m = jnp.max(logits, axis=-1, keepdims=True)
        e = jnp.exp(logits - m)
        p = e / jnp.sum(e, axis=-1, keepdims=True)
        cat_ref[:, cols] = _dot(p.astype(BF16), v_h).astype(BF16)
    o_ref[...] = x_ref[...] + _dot(cat_ref[...], wo_ref[...])


def _xattn(x, gain, wq, kv, wo):
    b, s, d = x.shape
    m = kv.shape[1]
    tm = XATTN_TM
    return pl.pallas_call(
        _xattn_kernel,
        grid=(b, s // tm),
        in_specs=[
            pl.BlockSpec((None, tm, d), lambda i, j: (i, j, 0)),
            _resident((1, d)),
            _resident((d, d)),
            pl.BlockSpec((None, m, 2 * d), lambda i, j: (i, 0, 0)),
            _resident((d, d)),
        ],
        out_specs=pl.BlockSpec((None, tm, d), lambda i, j: (i, j, 0)),
        out_shape=jax.ShapeDtypeStruct((b, s, d), F32),
        scratch_shapes=[pltpu.VMEM((tm, d), BF16)],
        compiler_params=pltpu.CompilerParams(
            dimension_semantics=("parallel", "parallel"),
            vmem_limit_bytes=VMEM_LIMIT_BYTES),
        name="xattn",
    )(x, gain, wq, kv, wo)


def _layer(x, mem, ffn1_norm, ffn1_w_gate, ffn1_w_up, ffn1_w_down, mix_norm, w_in, pool_w,
           pool_scale, gla_w_a2, gla_b_a, gla_head_norm, w_out, xattn_norm, mem_norm,
           xattn_w_q, xattn_w_kv, xattn_w_o, ffn2_norm, ffn2_w_gate, ffn2_w_up, ffn2_w_down,
           final_gain, *, final_norm):
    b, s, d = x.shape
    row = lambda v: v.reshape(1, -1)
    bf = lambda w: w.astype(BF16)

    x = x.reshape(b * s, d)
    n_tiles = x.shape[0] // FFN_TM

    x_head, w1, _ = _ffn(x, row(ffn1_norm), ffn1_w_gate, ffn1_w_up, ffn1_w_down, row(final_gain),
                         final_norm=False, name="ffn_head", tiles=(0, 1), tf=FFN_HEAD_TF)
    later_weights = (ffn2_w_gate, ffn2_w_up, ffn2_w_down, w_in, w_out, xattn_w_q, xattn_w_kv, xattn_w_o)
    side_casts = tuple((w, MAIN_COLS if w is w_in else w.shape[1]) for w in later_weights)
    x_tail, _, (w2g, w2u, w2d, w_main, w_o_mix, w_q, w_kv, w_o_att) = _ffn(
        x, row(ffn1_norm), *w1, row(final_gain), final_norm=False, name="ffn",
        tiles=(1, n_tiles - 1), side_casts=side_casts)

    w_a = jnp.pad(w_in[:, MAIN_COLS:], ((0, 0), (0, LANES - GLA_GATE_RANK)))
    w_a2 = jnp.pad(gla_w_a2, ((0, LANES - GLA_GATE_RANK), (0, 0)))
    x = _mixer(x_head, x_tail, b, row(mix_norm), w_main, bf(w_a), bf(w_a2), row(gla_b_a),
               bf(pool_w), row(pool_scale), row(gla_head_norm), w_o_mix)

    mb, ml, _ = mem.shape
    kv = _mem_kv(mem.reshape(mb * ml, d), row(mem_norm), w_kv).reshape(mb, ml, 2 * d)
    x = _xattn(x, row(xattn_norm), w_q, kv, w_o_att)

    x, _, _ = _ffn(x.reshape(b * s, d), row(ffn2_norm), w2g, w2u, w2d, row(final_gain),
                   final_norm=final_norm, name="ffn_final" if final_norm else "ffn_tail")
    return x.reshape(b, s, d)


def kernel(x, mem, ffn1_norm, ffn1_w_gate, ffn1_w_up, ffn1_w_down, mix_norm, w_in, pool_w, pool_scale, gla_w_a2, gla_b_a, gla_head_norm, w_out, xattn_norm, mem_norm, xattn_w_q, xattn_w_kv, xattn_w_o, ffn2_norm, ffn2_w_gate, ffn2_w_up, ffn2_w_down, final_norm):
    depth = ffn1_norm.shape[0]
    for l in range(depth):
        x = _layer(x, mem, ffn1_norm[l], ffn1_w_gate[l], ffn1_w_up[l], ffn1_w_down[l], mix_norm[l],
                   w_in[l], pool_w[l], pool_scale[l], gla_w_a2[l], gla_b_a[l], gla_head_norm[l],
                   w_out[l], xattn_norm[l], mem_norm[l], xattn_w_q[l], xattn_w_kv[l], xattn_w_o[l],
                   ffn2_norm[l], ffn2_w_gate[l], ffn2_w_up[l], ffn2_w_down[l], final_norm,
                   final_norm=(l == depth - 1))
    return x
```

```python
import functools

import jax
import jax.numpy as jnp
from jax import lax
from jax.experimental import pallas as pl
from jax.experimental.pallas import tpu as pltpu

F32 = jnp.float32
BF16 = jnp.bfloat16

D_MODEL = 2048
CHUNK = 64
D_POOL = 1024
POOL_WINDOWS = (2, 4, 8, 16)
POOL_GROUP = 256
POOL_HALO = 16
D_GLA = 1024
GLA_HEADS = 4
GLA_DV = 256
GLA_DK = 128
D_GLA_K = GLA_HEADS * GLA_DK
GLA_GATE_RANK = 16
GLA_GATE_TEMP = 16.0
XATTN_HEADS = 4
XATTN_HEAD_DIM = D_MODEL // XATTN_HEADS
RMS_EPS = 1e-6
LANES = 128
BF16_SUBLANES = 16

Q_OFF = D_POOL
K_OFF = Q_OFF + D_GLA_K
V_OFF = K_OFF + D_GLA_K
G_OFF = V_OFF + D_GLA
MAIN_COLS = G_OFF + D_GLA

VMEM_LIMIT_BYTES = 62 * 1024 * 1024

FFN_TM = 1024
FFN_TF = 512
FFN_ROW_CHUNK = 256
FFN_HEAD_TF = 256
MIX_TM = 256
XATTN_TM = 512
KV_TN = 1024


def _rms(x, gain):
    return x * lax.rsqrt(jnp.mean(x * x, axis=-1, keepdims=True) + RMS_EPS) * gain


def _dot(a, b):
    return jnp.dot(a, b, preferred_element_type=F32)


def _resident(shape):
    return pl.BlockSpec(shape, lambda *_: (0,) * len(shape), pipeline_mode=pl.Buffered(1))


def _ffn_kernel(x_ref, gain_ref, wg_ref, wu_ref, wd_ref, fgain_ref, *rest, final_norm, n_side,
                emit_weights):
    n_w = 3 if emit_weights else 0
    side_in, o_ref, h_ref = rest[:n_side], rest[n_side], rest[-1]
    w_out = rest[n_side + 1:n_side + 1 + n_w]
    side_out = rest[n_side + 1 + n_w:2 * n_side + 1 + n_w]

    def weight(ref, k):
        if not emit_weights:
            return ref[...]
        w = ref[...].astype(BF16)
        w_out[k][...] = w
        return w

    j = pl.program_id(1)
    last = pl.num_programs(1) - 1

    def swiglu_part(h, with_side_casts):
        g = _dot(h, weight(wg_ref, 0))
        if with_side_casts:
            for src_ref, dst_ref in zip(side_in, side_out):
                dst_ref[...] = src_ref[...].astype(BF16)
        u = _dot(h, weight(wu_ref, 1))
        a = (g * jax.nn.sigmoid(g) * u).astype(BF16)
        return _dot(a, weight(wd_ref, 2))

    def edge_step(first):
        for c in range(x_ref.shape[0] // FFN_ROW_CHUNK):
            rows = slice(c * FFN_ROW_CHUNK, (c + 1) * FFN_ROW_CHUNK)
            if first:
                h = _rms(x_ref[rows, :], gain_ref[...]).astype(BF16)
                h_ref[rows, :] = h
                o_ref[rows, :] = swiglu_part(h, c == 0)
            else:
                y = x_ref[rows, :] + 0.5 * (o_ref[rows, :] + swiglu_part(h_ref[rows, :], c == 0))
                if final_norm:
                    y = _rms(y, fgain_ref[...])
                o_ref[rows, :] = y

    @pl.when(j == 0)
    def _():
        edge_step(first=True)

    @pl.when(jnp.logical_and(j > 0, j < last))
    def _():
        o_ref[...] += swiglu_part(h_ref[...], True)

    @pl.when(j == last)
    def _():
        edge_step(first=False)


def _side_cast_block_rows(n_rows, n_steps):
    rows = BF16_SUBLANES
    while n_rows % rows or n_rows // rows > n_steps:
        rows += BF16_SUBLANES
    return rows


def _ffn(x, gain, wg, wu, wd, fgain, *, final_norm, name, tiles=None, tf=FFN_TF, side_casts=()):
    d = x.shape[1]
    dff = wg.shape[1]
    first, count = tiles if tiles is not None else (0, x.shape[0] // FFN_TM)
    t = count * FFN_TM
    emit_weights = wg.dtype == F32
    grid = (count, dff // tf)
    n_steps = grid[0] * grid[1]
    w_specs = [
        pl.BlockSpec((d, tf), lambda i, j: (0, j)),
        pl.BlockSpec((d, tf), lambda i, j: (0, j)),
        pl.BlockSpec((tf, d), lambda i, j: (j, 0)),
    ]
    w_out_specs = w_specs if emit_weights else []
    w_out_shapes = [jax.ShapeDtypeStruct(w.shape, BF16) for w in (wg, wu, wd)] if emit_weights else []
    side_specs_in, side_specs_out, side_shapes = [], [], []
    for w, n_cols in side_casts:
        rows = _side_cast_block_rows(w.shape[0], n_steps)
        n_blocks = w.shape[0] // rows
        index_map = functools.partial(
            lambda i, j, n_blocks: (jnp.minimum(i * grid[1] + j, n_blocks - 1), 0), n_blocks=n_blocks)
        side_specs_in.append(pl.BlockSpec((rows, n_cols), index_map))
        side_specs_out.append(pl.BlockSpec((rows, n_cols), index_map))
        side_shapes.append(jax.ShapeDtypeStruct((w.shape[0], n_cols), BF16))
    outs = pl.pallas_call(
        functools.partial(_ffn_kernel, final_norm=final_norm, n_side=len(side_casts),
                          emit_weights=emit_weights),
        grid=grid,
        in_specs=[
            pl.BlockSpec((FFN_TM, d), lambda i, j: (first + i, 0)),
            pl.BlockSpec((1, d), lambda i, j: (0, 0)),
            *w_specs,
            pl.BlockSpec((1, d), lambda i, j: (0, 0)),
        ] + side_specs_in,
        out_specs=[pl.BlockSpec((FFN_TM, d), lambda i, j: (i, 0))] + w_out_specs + side_specs_out,
        out_shape=[jax.ShapeDtypeStruct((t, d), F32)] + w_out_shapes + side_shapes,
        scratch_shapes=[pltpu.VMEM((FFN_TM, d), BF16)],
        compiler_params=pltpu.CompilerParams(
            dimension_semantics=("arbitrary", "arbitrary"),
            vmem_limit_bytes=VMEM_LIMIT_BYTES),
        name=name,
    )(x, gain, wg, wu, wd, fgain, *[w for w, _ in side_casts])
    n_w = len(w_out_shapes)
    return outs[0], tuple(outs[1:1 + n_w]), tuple(outs[1 + n_w:])


def _mix_kernel(xh_ref, xt_ref, gain_ref, win_ref, wa_ref, wa2_ref, ba_ref, poolw_ref, pscale_ref,
                hnorm_ref, wout_ref, o_ref,
                proj_ref, oacc_ref, ubuf_ref, cat_ref, state_ref, *, n_head_tiles):
    tm = o_ref.shape[0]
    s = pl.program_id(1)
    in_head = pl.program_id(0) * pl.num_programs(1) + s < n_head_tiles

    def x_tile():
        return jnp.where(in_head, xh_ref[...], xt_ref[...])

    @pl.when(s == 0)
    def _():
        state_ref[...] = jnp.zeros_like(state_ref)
        ubuf_ref[0:POOL_HALO, :] = jnp.zeros((POOL_HALO, D_POOL), F32)

    h = _rms(x_tile(), gain_ref[...]).astype(BF16)
    proj_ref[...] = _dot(h, win_ref[...])
    a_lr = _dot(h, wa_ref[...])
    z = _dot(a_lr.astype(BF16), wa2_ref[...]) + ba_ref[...]
    la = (jnp.minimum(z, 0.0) - jnp.log1p(jnp.exp(-jnp.abs(z)))) * (1.0 / GLA_GATE_TEMP)

    ubuf_ref[POOL_HALO:POOL_HALO + tm, :] = proj_ref[:, 0:D_POOL]
    t_idx = s * tm + lax.broadcasted_iota(jnp.int32, (tm, 1), 0)
    count = (t_idx + 1).astype(F32)
    for gi, w in enumerate(POOL_WINDOWS):
        cols = slice(gi * POOL_GROUP, (gi + 1) * POOL_GROUP)
        ext = ubuf_ref[:, cols]
        acc = ext
        shift = 1
        while shift < w:
            acc = acc + pltpu.roll(acc, shift, 0)
            shift *= 2
        win = acc[POOL_HALO:, :]
        d = win / jnp.minimum(count, float(w)) - ext[POOL_HALO:, :]
        y = _dot(d.astype(BF16), poolw_ref[gi]) * pscale_ref[:, cols]
        cat_ref[:, cols] = y.astype(BF16)
    ubuf_ref[0:POOL_HALO, :] = ubuf_ref[tm:tm + POOL_HALO, :]

    row = lax.broadcasted_iota(jnp.int32, (tm, tm), 0)
    col = lax.broadcasted_iota(jnp.int32, (tm, tm), 1)
    same_chunk = (row // CHUNK) == (col // CHUNK)
    chunk_sum = same_chunk.astype(BF16)
    chunk_prefix = (same_chunk & (row >= col)).astype(BF16)
    la_hi = la.astype(BF16)
    la_lo = (la - la_hi.astype(F32)).astype(BF16)
    cum = _dot(chunk_prefix, la_hi) + _dot(chunk_prefix, la_lo)
    b_end = _dot(chunk_sum, la_hi) + _dot(chunk_sum, la_lo)
    k_dec = (proj_ref[:, K_OFF:K_OFF + D_GLA_K] * jnp.exp(b_end - cum)).astype(BF16)
    decay = jnp.exp(b_end)
    q = (proj_ref[:, Q_OFF:Q_OFF + D_GLA_K] * (GLA_DK ** -0.5)).astype(BF16)
    n_chunks = tm // CHUNK
    heads_chunks = [(hd, c) for hd in range(GLA_HEADS) for c in range(n_chunks)]
    upd = {}
    for hd, c in heads_chunks:
        rows = slice(c * CHUNK, (c + 1) * CHUNK)
        v_h = proj_ref[rows, V_OFF + hd * GLA_DV:V_OFF + (hd + 1) * GLA_DV].astype(BF16)
        upd[hd, c] = lax.dot_general(v_h, k_dec[rows, hd * GLA_DK:(hd + 1) * GLA_DK],
                                     (((0,), (0,)), ((), ())),
                                     preferred_element_type=F32)
    states = {}
    for hd in range(GLA_HEADS):
        st = state_ref[hd]
        for c in range(n_chunks):
            st = st * decay[c * CHUNK:c * CHUNK + 1, hd * GLA_DK:(hd + 1) * GLA_DK] + upd[hd, c]
            states[hd, c] = st.astype(BF16)
        state_ref[hd] = st
    for hd, c in heads_chunks:
        rows = slice(c * CHUNK, (c + 1) * CHUNK)
        oacc_ref[rows, hd * GLA_DV:(hd + 1) * GLA_DV] = lax.dot_general(
            q[rows, hd * GLA_DK:(hd + 1) * GLA_DK], states[hd, c], (((1,), (1,)), ((), ())),
            preferred_element_type=F32)
    for hd in range(GLA_HEADS):
        hc = slice(hd * GLA_DV, (hd + 1) * GLA_DV)
        o = oacc_ref[:, hc]
        o = o * lax.rsqrt(jnp.mean(o * o, axis=-1, keepdims=True) + RMS_EPS) * hnorm_ref[:, hc]
        g_h = proj_ref[:, G_OFF + hd * GLA_DV:G_OFF + (hd + 1) * GLA_DV]
        y = o * (g_h * jax.nn.sigmoid(g_h))
        cat_ref[:, D_POOL + hd * GLA_DV:D_POOL + (hd + 1) * GLA_DV] = y.astype(BF16)

    o_ref[...] = x_tile() + _dot(cat_ref[...], wout_ref[...])


def _mixer(x_head, x_tail, b, gain, win, wa, wa2, ba, poolw, pscale, hnorm, wout):
    d = x_head.shape[1]
    tm = MIX_TM
    n_head = x_head.shape[0] // tm
    s = (x_head.shape[0] + x_tail.shape[0]) // b
    n_s = s // tm
    return pl.pallas_call(
        functools.partial(_mix_kernel, n_head_tiles=n_head),
        grid=(b, n_s),
        in_specs=[
            pl.BlockSpec((tm, d), lambda i, j: (jnp.minimum(i * n_s + j, n_head - 1), 0)),
            pl.BlockSpec((tm, d), lambda i, j: (jnp.maximum(i * n_s + j - n_head, 0), 0)),
            _resident((1, d)),
            _resident((d, MAIN_COLS)),
            _resident((d, LANES)),
            _resident((LANES, D_GLA_K)),
            _resident((1, D_GLA_K)),
            _resident((len(POOL_WINDOWS), POOL_GROUP, POOL_GROUP)),
            _resident((1, D_POOL)),
            _resident((1, D_GLA)),
            _resident((d, d)),
        ],
        out_specs=pl.BlockSpec((None, tm, d), lambda i, j: (i, j, 0)),
        out_shape=jax.ShapeDtypeStruct((b, s, d), F32),
        scratch_shapes=[
            pltpu.VMEM((tm, MAIN_COLS), F32),
            pltpu.VMEM((tm, D_GLA), F32),
            pltpu.VMEM((POOL_HALO + tm, D_POOL), F32),
            pltpu.VMEM((tm, d), BF16),
            pltpu.VMEM((GLA_HEADS, GLA_DV, GLA_DK), F32),
        ],
        compiler_params=pltpu.CompilerParams(
            dimension_semantics=("arbitrary", "arbitrary"),
            vmem_limit_bytes=VMEM_LIMIT_BYTES),
        name="mixer",
    )(x_head, x_tail, gain, win, wa, wa2, ba, poolw, pscale, hnorm, wout)


def _kv_kernel(mem_ref, gain_ref, wkv_ref, o_ref):
    h = _rms(mem_ref[...], gain_ref[...]).astype(BF16)
    o_ref[...] = _dot(h, wkv_ref[...]).astype(BF16)


def _mem_kv(mem, gain, wkv):
    t, d = mem.shape
    n = wkv.shape[1]
    return pl.pallas_call(
        _kv_kernel,
        grid=(n // KV_TN,),
        in_specs=[
            pl.BlockSpec((t, d), lambda j: (0, 0)),
            pl.BlockSpec((1, d), lambda j: (0, 0)),
            pl.BlockSpec((d, KV_TN), lambda j: (0, j)),
        ],
        out_specs=pl.BlockSpec((t, KV_TN), lambda j: (0, j)),
        out_shape=jax.ShapeDtypeStruct((t, n), BF16),
        compiler_params=pltpu.CompilerParams(
            dimension_semantics=("arbitrary",),
            vmem_limit_bytes=VMEM_LIMIT_BYTES),
        name="mem_kv",
    )(mem, gain, wkv)


def _xattn_kernel(x_ref, gain_ref, wq_ref, kv_ref, wo_ref, o_ref, cat_ref):
    h = _rms(x_ref[...], gain_ref[...]).astype(BF16)
    q = (_dot(h, wq_ref[...]) * (XATTN_HEAD_DIM ** -0.5)).astype(BF16)
    for hd in range(XATTN_HEADS):
        cols = slice(hd * XATTN_HEAD_DIM, (hd + 1) * XATTN_HEAD_DIM)
        k_h = kv_ref[:, cols]
        v_h = kv_ref[:, D_MODEL + hd * XATTN_HEAD_DIM:D_MODEL + (hd + 1) * XATTN_HEAD_DIM]
        logits = lax.dot_general(q[:, cols], k_h, (((1,), (1,)), ((), ())),
                                 preferred_element_type=F32)
        m = jnp.max(logits, axis=-1, keepdims=True)
        e = jnp.exp(logits - m)
        p = e / jnp.sum(e, axis=-1, keepdims=True)
        cat_ref[:, cols] = _dot(p.astype(BF16), v_h).astype(BF16)
    o_ref[...] = x_ref[...] + _dot(cat_ref[...], wo_ref[...])


def _xattn(x, gain, wq, kv, wo):
    b, s, d = x.shape
    m = kv.shape[1]
    tm = XATTN_TM
    return pl.pallas_call(
        _xattn_kernel,
        grid=(b, s // tm),
        in_specs=[
            pl.BlockSpec((None, tm, d), lambda i, j: (i, j, 0)),
            _resident((1, d)),
            _resident((d, d)),
            pl.BlockSpec((None, m, 2 * d), lambda i, j: (i, 0, 0)),
            _resident((d, d)),
        ],
        out_specs=pl.BlockSpec((None, tm, d), lambda i, j: (i, j, 0)),
        out_shape=jax.ShapeDtypeStruct((b, s, d), F32),
        scratch_shapes=[pltpu.VMEM((tm, d), BF16)],
        compiler_params=pltpu.CompilerParams(
            dimension_semantics=("parallel", "parallel"),
            vmem_limit_bytes=VMEM_LIMIT_BYTES),
        name="xattn",
    )(x, gain, wq, kv, wo)


def _layer(x, mem, ffn1_norm, ffn1_w_gate, ffn1_w_up, ffn1_w_down, mix_norm, w_in, pool_w,
           pool_scale, gla_w_a2, gla_b_a, gla_head_norm, w_out, xattn_norm, mem_norm,
           xattn_w_q, xattn_w_kv, xattn_w_o, ffn2_norm, ffn2_w_gate, ffn2_w_up, ffn2_w_down,
           final_gain, *, final_norm):
    b, s, d = x.shape
    row = lambda v: v.reshape(1, -1)
    bf = lambda w: w.astype(BF16)

    x = x.reshape(b * s, d)
    n_tiles = x.shape[0] // FFN_TM

    x_head, w1, _ = _ffn(x, row(ffn1_norm), ffn1_w_gate, ffn1_w_up, ffn1_w_down, row(final_gain),
                         final_norm=False, name="ffn_head", tiles=(0, 1), tf=FFN_HEAD_TF)
    later_weights = (ffn2_w_gate, ffn2_w_up, ffn2_w_down, w_in, w_out, xattn_w_q, xattn_w_kv, xattn_w_o)
    side_casts = tuple((w, MAIN_COLS if w is w_in else w.shape[1]) for w in later_weights)
    x_tail, _, (w2g, w2u, w2d, w_main, w_o_mix, w_q, w_kv, w_o_att) = _ffn(
        x, row(ffn1_norm), *w1, row(final_gain), final_norm=False, name="ffn",
        tiles=(1, n_tiles - 1), side_casts=side_casts)

    w_a = jnp.pad(w_in[:, MAIN_COLS:], ((0, 0), (0, LANES - GLA_GATE_RANK)))
    w_a2 = jnp.pad(gla_w_a2, ((0, LANES - GLA_GATE_RANK), (0, 0)))
    x = _mixer(x_head, x_tail, b, row(mix_norm), w_main, bf(w_a), bf(w_a2), row(gla_b_a),
               bf(pool_w), row(pool_scale), row(gla_head_norm), w_o_mix)

    mb, ml, _ = mem.shape
    kv = _mem_kv(mem.reshape(mb * ml, d), row(mem_norm), w_kv).reshape(mb, ml, 2 * d)
    x = _xattn(x, row(xattn_norm), w_q, kv, w_o_att)

    x, _, _ = _ffn(x.reshape(b * s, d), row(ffn2_norm), w2g, w2u, w2d, row(final_gain),
                   final_norm=final_norm, name="ffn_final" if final_norm else "ffn_tail")
    return x.reshape(b, s, d)


def kernel(x, mem, ffn1_norm, ffn1_w_gate, ffn1_w_up, ffn1_w_down, mix_norm, w_in, pool_w, pool_scale, gla_w_a2, gla_b_a, gla_head_norm, w_out, xattn_norm, mem_norm, xattn_w_q, xattn_w_kv, xattn_w_o, ffn2_norm, ffn2_w_gate, ffn2_w_up, ffn2_w_down, final_norm):
    depth = ffn1_norm.shape[0]
    for l in range(depth):
        x = _layer(x, mem, ffn1_norm[l], ffn1_w_gate[l], ffn1_w_up[l], ffn1_w_down[l], mix_norm[l],
                   w_in[l], pool_w[l], pool_scale[l], gla_w_a2[l], gla_b_a[l], gla_head_norm[l],
                   w_out[l], xattn_norm[l], mem_norm[l], xattn_w_q[l], xattn_w_kv[l], xattn_w_o[l],
                   ffn2_norm[l], ffn2_w_gate[l], ffn2_w_up[l], ffn2_w_down[l], final_norm,
                   final_norm=(l == depth - 1))
    return x
```

```python
import functools

import jax
import jax.numpy as jnp
from jax import lax
from jax.experimental import pallas as pl
from jax.experimental.pallas import tpu as pltpu

F32 = jnp.float32
BF16 = jnp.bfloat16

D_MODEL = 2048
CHUNK = 64
D_POOL = 1024
POOL_WINDOWS = (2, 4, 8, 16)
POOL_GROUP = 256
POOL_HALO = 16
D_GLA = 1024
GLA_HEADS = 4
GLA_DV = 256
GLA_DK = 128
D_GLA_K = GLA_HEADS * GLA_DK
GLA_GATE_RANK = 16
GLA_GATE_TEMP = 16.0
XATTN_HEADS = 4
XATTN_HEAD_DIM = D_MODEL // XATTN_HEADS
RMS_EPS = 1e-6
LANES = 128
BF16_SUBLANES = 16

Q_OFF = D_POOL
K_OFF = Q_OFF + D_GLA_K
V_OFF = K_OFF + D_GLA_K
G_OFF = V_OFF + D_GLA
MAIN_COLS = G_OFF + D_GLA

VMEM_LIMIT_BYTES = 62 * 1024 * 1024

FFN_TM = 1024
FFN_TF = 512
FFN_ROW_CHUNK = 256
FFN_HEAD_TF = 256
MIX_TM = 256
XATTN_TM = 512
KV_TN = 1024


def _rms(x, gain):
    return x * lax.rsqrt(jnp.mean(x * x, axis=-1, keepdims=True) + RMS_EPS) * gain


def _dot(a, b):
    return jnp.dot(a, b, preferred_element_type=F32)


def _resident(shape):
    return pl.BlockSpec(shape, lambda *_: (0,) * len(shape), pipeline_mode=pl.Buffered(1))


def _ffn_kernel(x_ref, gain_ref, wg_ref, wu_ref, wd_ref, fgain_ref, *rest, final_norm, n_side,
                emit_weights):
    n_w = 3 if emit_weights else 0
    side_in, o_ref, h_ref = rest[:n_side], rest[n_side], rest[-1]
    w_out = rest[n_side + 1:n_side + 1 + n_w]
    side_out = rest[n_side + 1 + n_w:2 * n_side + 1 + n_w]

    def weight(ref, k):
        if not emit_weights:
            return ref[...]
        w = ref[...].astype(BF16)
        w_out[k][...] = w
        return w

    j = pl.program_id(1)
    last = pl.num_programs(1) - 1

    def swiglu_part(h, with_side_casts):
        g = _dot(h, weight(wg_ref, 0))
        if with_side_casts:
            for src_ref, dst_ref in zip(side_in, side_out):
                dst_ref[...] = src_ref[...].astype(BF16)
        u = _dot(h, weight(wu_ref, 1))
        a = (g * jax.nn.sigmoid(g) * u).astype(BF16)
        return _dot(a, weight(wd_ref, 2))

    def edge_step(first):
        for c in range(x_ref.shape[0] // FFN_ROW_CHUNK):
            rows = slice(c * FFN_ROW_CHUNK, (c + 1) * FFN_ROW_CHUNK)
            if first:
                x = x_ref[rows, :]
                h = _rms(x, gain_ref[...]).astype(BF16)
                h_ref[rows, :] = h
                o_ref[rows, :] = 2.0 * x + swiglu_part(h, c == 0)
            else:
                y = 0.5 * (o_ref[rows, :] + swiglu_part(h_ref[rows, :], c == 0))
                if final_norm:
                    y = _rms(y, fgain_ref[...])
                o_ref[rows, :] = y

    @pl.when(j == 0)
    def _():
        edge_step(first=True)

    @pl.when(jnp.logical_and(j > 0, j < last))
    def _():
        o_ref[...] += swiglu_part(h_ref[...], True)

    @pl.when(j == last)
    def _():
        edge_step(first=False)


def _side_cast_block_rows(n_rows, n_steps):
    rows = BF16_SUBLANES
    while n_rows % rows or n_rows // rows > n_steps:
        rows += BF16_SUBLANES
    return rows


def _ffn(x, gain, wg, wu, wd, fgain, *, final_norm, name, tiles=None, tf=FFN_TF, side_casts=()):
    d = x.shape[1]
    dff = wg.shape[1]
    first, count = tiles if tiles is not None else (0, x.shape[0] // FFN_TM)
    t = count * FFN_TM
    emit_weights = wg.dtype == F32
    grid = (count, dff // tf)
    n_steps = grid[0] * grid[1]
    w_specs = [
        pl.BlockSpec((d, tf), lambda i, j: (0, j)),
        pl.BlockSpec((d, tf), lambda i, j: (0, j)),
        pl.BlockSpec((tf, d), lambda i, j: (j, 0)),
    ]

    def x_index(i, j):
        ahead = (j >= grid[1] // 2).astype(jnp.int32)
        return (first + jnp.minimum(i + ahead, count - 1), 0)

    w_out_specs = w_specs if emit_weights else []
    w_out_shapes = [jax.ShapeDtypeStruct(w.shape, BF16) for w in (wg, wu, wd)] if emit_weights else []
    side_specs_in, side_specs_out, side_shapes = [], [], []
    for w, n_rows, n_cols in side_casts:
        rows = _side_cast_block_rows(n_rows, n_steps)
        index_map = functools.partial(
            lambda i, j, n_blocks: (jnp.minimum(i * grid[1] + j, n_blocks - 1), 0),
            n_blocks=n_rows // rows)
        side_specs_in.append(pl.BlockSpec((rows, n_cols), index_map))
        side_specs_out.append(pl.BlockSpec((rows, n_cols), index_map))
        side_shapes.append(jax.ShapeDtypeStruct((n_rows, n_cols), BF16))
    outs = pl.pallas_call(
        functools.partial(_ffn_kernel, final_norm=final_norm, n_side=len(side_casts),
                          emit_weights=emit_weights),
        grid=grid,
        in_specs=[
            pl.BlockSpec((FFN_TM, d), x_index),
            pl.BlockSpec((1, d), lambda i, j: (0, 0)),
            *w_specs,
            pl.BlockSpec((1, d), lambda i, j: (0, 0)),
        ] + side_specs_in,
        out_specs=[pl.BlockSpec((FFN_TM, d), lambda i, j: (i, 0))] + w_out_specs + side_specs_out,
        out_shape=[jax.ShapeDtypeStruct((t, d), F32)] + w_out_shapes + side_shapes,
        scratch_shapes=[pltpu.VMEM((FFN_TM, d), BF16)],
        compiler_params=pltpu.CompilerParams(
            dimension_semantics=("arbitrary", "arbitrary"),
            vmem_limit_bytes=VMEM_LIMIT_BYTES),
        name=name,
    )(x, gain, wg, wu, wd, fgain, *[w for w, _, _ in side_casts])
    n_w = len(w_out_shapes)
    return outs[0], tuple(outs[1:1 + n_w]), tuple(outs[1 + n_w:])


def _mix_kernel(xh_ref, xt_ref, gain_ref, win_ref, wa_ref, wa2_ref, ba_ref, poolw_ref, pscale_ref,
                hnorm_ref, wout_ref, o_ref,
                proj_ref, oacc_ref, ubuf_ref, cat_ref, state_ref, *, n_head_tiles):
    tm = o_ref.shape[0]
    s = pl.program_id(1)
    in_head = pl.program_id(0) * pl.num_programs(1) + s < n_head_tiles

    def x_tile():
        return jnp.where(in_head, xh_ref[...], xt_ref[...])

    @pl.when(s == 0)
    def _():
        state_ref[...] = jnp.zeros_like(state_ref)
        ubuf_ref[0:POOL_HALO, :] = jnp.zeros((POOL_HALO, D_POOL), F32)

    h = _rms(x_tile(), gain_ref[...]).astype(BF16)
    proj_ref[...] = lax.dot_general(h, win_ref[...], (((1,), (1,)), ((), ())),
                                    preferred_element_type=F32)
    a_lr = _dot(h, wa_ref[...])
    z = _dot(a_lr.astype(BF16), wa2_ref[...]) + ba_ref[...]
    la = (jnp.minimum(z, 0.0) - jnp.log1p(jnp.exp(-jnp.abs(z)))) * (1.0 / GLA_GATE_TEMP)

    ubuf_ref[POOL_HALO:POOL_HALO + tm, :] = proj_ref[:, 0:D_POOL]
    t_idx = s * tm + lax.broadcasted_iota(jnp.int32, (tm, 1), 0)
    count = (t_idx + 1).astype(F32)
    for gi, w in enumerate(POOL_WINDOWS):
        cols = slice(gi * POOL_GROUP, (gi + 1) * POOL_GROUP)
        ext = ubuf_ref[:, cols]
        acc = ext
        shift = 1
        while shift < w:
            acc = acc + pltpu.roll(acc, shift, 0)
            shift *= 2
        win = acc[POOL_HALO:, :]
        d = win / jnp.minimum(count, float(w)) - ext[POOL_HALO:, :]
        y = _dot(d.astype(BF16), poolw_ref[gi]) * pscale_ref[:, cols]
        cat_ref[:, cols] = y.astype(BF16)
    ubuf_ref[0:POOL_HALO, :] = ubuf_ref[tm:tm + POOL_HALO, :]

    row = lax.broadcasted_iota(jnp.int32, (tm, tm), 0)
    col = lax.broadcasted_iota(jnp.int32, (tm, tm), 1)
    same_chunk = (row // CHUNK) == (col // CHUNK)
    chunk_sum = same_chunk.astype(BF16)
    chunk_prefix = (same_chunk & (row >= col)).astype(BF16)
    la_hi = la.astype(BF16)
    la_lo = (la - la_hi.astype(F32)).astype(BF16)
    cum = _dot(chunk_prefix, la_hi) + _dot(chunk_prefix, la_lo)
    b_end = _dot(chunk_sum, la_hi) + _dot(chunk_sum, la_lo)
    k_dec = (proj_ref[:, K_OFF:K_OFF + D_GLA_K] * jnp.exp(b_end - cum)).astype(BF16)
    decay = jnp.exp(b_end)
    q = (proj_ref[:, Q_OFF:Q_OFF + D_GLA_K] * (GLA_DK ** -0.5)).astype(BF16)
    n_chunks = tm // CHUNK
    heads_chunks = [(hd, c) for hd in range(GLA_HEADS) for c in range(n_chunks)]
    upd = {}
    for hd, c in heads_chunks:
        rows = slice(c * CHUNK, (c + 1) * CHUNK)
        v_h = proj_ref[rows, V_OFF + hd * GLA_DV:V_OFF + (hd + 1) * GLA_DV].astype(BF16)
        upd[hd, c] = lax.dot_general(v_h, k_dec[rows, hd * GLA_DK:(hd + 1) * GLA_DK],
                                     (((0,), (0,)), ((), ())),
                                     preferred_element_type=F32)
    states = {}
    for hd in range(GLA_HEADS):
        st = state_ref[hd]
        for c in range(n_chunks):
            st = st * decay[c * CHUNK:c * CHUNK + 1, hd * GLA_DK:(hd + 1) * GLA_DK] + upd[hd, c]
            states[hd, c] = st.astype(BF16)
        state_ref[hd] = st
    for hd, c in heads_chunks:
        rows = slice(c * CHUNK, (c + 1) * CHUNK)
        oacc_ref[rows, hd * GLA_DV:(hd + 1) * GLA_DV] = lax.dot_general(
            q[rows, hd * GLA_DK:(hd + 1) * GLA_DK], states[hd, c], (((1,), (1,)), ((), ())),
            preferred_element_type=F32)
    for hd in range(GLA_HEADS):
        hc = slice(hd * GLA_DV, (hd + 1) * GLA_DV)
        o = oacc_ref[:, hc]
        o = o * lax.rsqrt(jnp.mean(o * o, axis=-1, keepdims=True) + RMS_EPS) * hnorm_ref[:, hc]
        g_h = proj_ref[:, G_OFF + hd * GLA_DV:G_OFF + (hd + 1) * GLA_DV]
        y = o * (g_h * jax.nn.sigmoid(g_h))
        cat_ref[:, D_POOL + hd * GLA_DV:D_POOL + (hd + 1) * GLA_DV] = y.astype(BF16)

    o_ref[...] = x_tile() + _dot(cat_ref[...], wout_ref[...])


def _mixer(x_head, x_tail, b, gain, win, wa, wa2, ba, poolw, pscale, hnorm, wout):
    d = x_head.shape[1]
    tm = MIX_TM
    n_head = x_head.shape[0] // tm
    s = (x_head.shape[0] + x_tail.shape[0]) // b
    n_s = s // tm
    return pl.pallas_call(
        functools.partial(_mix_kernel, n_head_tiles=n_head),
        grid=(b, n_s),
        in_specs=[
            pl.BlockSpec((tm, d), lambda i, j: (jnp.minimum(i * n_s + j, n_head - 1), 0)),
            pl.BlockSpec((tm, d), lambda i, j: (jnp.maximum(i * n_s + j - n_head, 0), 0)),
            _resident((1, d)),
            _resident((MAIN_COLS, d)),
            _resident((d, LANES)),
            _resident((LANES, D_GLA_K)),
            _resident((1, D_GLA_K)),
            _resident((len(POOL_WINDOWS), POOL_GROUP, POOL_GROUP)),
            _resident((1, D_POOL)),
            _resident((1, D_GLA)),
            _resident((d, d)),
        ],
        out_specs=pl.BlockSpec((None, tm, d), lambda i, j: (i, j, 0)),
        out_shape=jax.ShapeDtypeStruct((b, s, d), F32),
        scratch_shapes=[
            pltpu.VMEM((tm, MAIN_COLS), F32),
            pltpu.VMEM((tm, D_GLA), F32),
            pltpu.VMEM((POOL_HALO + tm, D_POOL), F32),
            pltpu.VMEM((tm, d), BF16),
            pltpu.VMEM((GLA_HEADS, GLA_DV, GLA_DK), F32),
        ],
        compiler_params=pltpu.CompilerParams(
            dimension_semantics=("arbitrary", "arbitrary"),
            vmem_limit_bytes=VMEM_LIMIT_BYTES),
        name="mixer",
    )(x_head, x_tail, gain, win, wa, wa2, ba, poolw, pscale, hnorm, wout)


def _kv_kernel(mem_ref, gain_ref, wkv_ref, o_ref):
    h = _rms(mem_ref[...], gain_ref[...]).astype(BF16)
    o_ref[...] = _dot(h, wkv_ref[...]).astype(BF16)


def _mem_kv(mem, gain, wkv):
    t, d = mem.shape
    n = wkv.shape[1]
    return pl.pallas_call(
        _kv_kernel,
        grid=(n // KV_TN,),
        in_specs=[
            pl.BlockSpec((t, d), lambda j: (0, 0)),
            pl.BlockSpec((1, d), lambda j: (0, 0)),
            pl.BlockSpec((d, KV_TN), lambda j: (0, j)),
        ],
        out_specs=pl.BlockSpec((t, KV_TN), lambda j: (0, j)),
        out_shape=jax.ShapeDtypeStruct((t, n), BF16),
        compiler_params=pltpu.CompilerParams(
            dimension_semantics=("arbitrary",),
            vmem_limit_bytes=VMEM_LIMIT_BYTES),
        name="mem_kv",
    )(mem, gain, wkv)


def _xattn_kernel(x_ref, gain_ref, wq_ref, kv_ref, wo_ref, o_ref, cat_ref):
    h = _rms(x_ref[...], gain_ref[...]).astype(BF16)
    q = (_dot(h, wq_ref[...]) * (XATTN_HEAD_DIM ** -0.5)).astype(BF16)
    for hd in range(XATTN_HEADS):
        cols = slice(hd * XATTN_HEAD_DIM, (hd + 1) * XATTN_HEAD_DIM)
        k_h = kv_ref[:, cols]
        v_h = kv_ref[:, D_MODEL + hd * XATTN_HEAD_DIM:D_MODEL + (hd + 1) * XATTN_HEAD_DIM]
        logits = lax.dot_general(q[:, cols], k_h, (((1,), (1,)), ((), ())),
                                 preferred_element_type=F32)
        m = jnp.max(logits, axis=-1, keepdims=True)
        e = jnp.exp(logits - m)
        p = e / jnp.sum(e, axis=-1, keepdims=True)
        cat_ref[:, cols] = _dot(p.astype(BF16), v_h).astype(BF16)
    o_ref[...] = x_ref[...] + _dot(cat_ref[...], wo_ref[...])


def _xattn(x, gain, wq, kv, wo):
    b, s, d = x.shape
    m = kv.shape[1]
    tm = XATTN_TM
    return pl.pallas_call(
        _xattn_kernel,
        grid=(b, s // tm),
        in_specs=[
            pl.BlockSpec((None, tm, d), lambda i, j: (i, j, 0)),
            _resident((1, d)),
            _resident((d, d)),
            pl.BlockSpec((None, m, 2 * d), lambda i, j: (i, 0, 0)),
            _resident((d, d)),
        ],
        out_specs=pl.BlockSpec((None, tm, d), lambda i, j: (i, j, 0)),
        out_shape=jax.ShapeDtypeStruct((b, s, d), F32),
        scratch_shapes=[pltpu.VMEM((tm, d), BF16)],
        compiler_params=pltpu.CompilerParams(
            dimension_semantics=("parallel", "parallel"),
            vmem_limit_bytes=VMEM_LIMIT_BYTES),
        name="xattn",
    )(x, gain, wq, kv, wo)


def _layer(x, mem, ffn1_norm, ffn1_w_gate, ffn1_w_up, ffn1_w_down, mix_norm, w_in, pool_w,
           pool_scale, gla_w_a2, gla_b_a, gla_head_norm, w_out, xattn_norm, mem_norm,
           xattn_w_q, xattn_w_kv, xattn_w_o, ffn2_norm, ffn2_w_gate, ffn2_w_up, ffn2_w_down,
           final_gain, *, final_norm):
    b, s, d = x.shape
    row = lambda v: v.reshape(1, -1)
    bf = lambda w: w.astype(BF16)

    x = x.reshape(b * s, d)
    n_tiles = x.shape[0] // FFN_TM

    x_head, w1, _ = _ffn(x, row(ffn1_norm), ffn1_w_gate, ffn1_w_up, ffn1_w_down, row(final_gain),
                         final_norm=False, name="ffn_head", tiles=(0, 1), tf=FFN_HEAD_TF)
    w_in_t = w_in.T
    later_weights = (ffn2_w_gate, ffn2_w_up, ffn2_w_down, w_out, xattn_w_q, xattn_w_kv, xattn_w_o)
    side_casts = tuple((w, *w.shape) for w in later_weights) + ((w_in_t, MAIN_COLS, d),)
    x_tail, _, (w2g, w2u, w2d, w_o_mix, w_q, w_kv, w_o_att, w_main) = _ffn(
        x, row(ffn1_norm), *w1, row(final_gain), final_norm=False, name="ffn",
        tiles=(1, n_tiles - 1), side_casts=side_casts)

    w_a = jnp.pad(w_in_t[MAIN_COLS:, :].T, ((0, 0), (0, LANES - GLA_GATE_RANK)))
    w_a2 = jnp.pad(gla_w_a2, ((0, LANES - GLA_GATE_RANK), (0, 0)))
    x = _mixer(x_head, x_tail, b, row(mix_norm), w_main, bf(w_a), bf(w_a2), row(gla_b_a),
               bf(pool_w), row(pool_scale), row(gla_head_norm), w_o_mix)

    mb, ml, _ = mem.shape
    kv = _mem_kv(mem.reshape(mb * ml, d), row(mem_norm), w_kv).reshape(mb, ml, 2 * d)
    x = _xattn(x, row(xattn_norm), w_q, kv, w_o_att)

    x, _, _ = _ffn(x.reshape(b * s, d), row(ffn2_norm), w2g, w2u, w2d, row(final_gain),
                   final_norm=final_norm, name="ffn_final" if final_norm else "ffn_tail")
    return x.reshape(b, s, d)


def kernel(x, mem, ffn1_norm, ffn1_w_gate, ffn1_w_up, ffn1_w_down, mix_norm, w_in, pool_w, pool_scale, gla_w_a2, gla_b_a, gla_head_norm, w_out, xattn_norm, mem_norm, xattn_w_q, xattn_w_kv, xattn_w_o, ffn2_norm, ffn2_w_gate, ffn2_w_up, ffn2_w_down, final_norm):
    depth = ffn1_norm.shape[0]
    for l in range(depth):
        x = _layer(x, mem, ffn1_norm[l], ffn1_w_gate[l], ffn1_w_up[l], ffn1_w_down[l], mix_norm[l],
                   w_in[l], pool_w[l], pool_scale[l], gla_w_a2[l], gla_b_a[l], gla_head_norm[l],
                   w_out[l], xattn_norm[l], mem_norm[l], xattn_w_q[l], xattn_w_kv[l], xattn_w_o[l],
                   ffn2_norm[l], ffn2_w_gate[l], ffn2_w_up[l], ffn2_w_down[l], final_norm,
                   final_norm=(l == depth - 1))
    return x
```

```python
import functools

import jax
import jax.numpy as jnp
from jax import lax
from jax.experimental import pallas as pl
from jax.experimental.pallas import tpu as pltpu

F32 = jnp.float32
BF16 = jnp.bfloat16

D_MODEL = 2048
CHUNK = 64
D_POOL = 1024
POOL_WINDOWS = (2, 4, 8, 16)
POOL_GROUP = 256
POOL_HALO = 16
D_GLA = 1024
GLA_HEADS = 4
GLA_DV = 256
GLA_DK = 128
D_GLA_K = GLA_HEADS * GLA_DK
GLA_GATE_RANK = 16
GLA_GATE_TEMP = 16.0
XATTN_HEADS = 4
XATTN_HEAD_DIM = D_MODEL // XATTN_HEADS
RMS_EPS = 1e-6
LANES = 128
BF16_SUBLANES = 16

Q_OFF = D_POOL
K_OFF = Q_OFF + D_GLA_K
V_OFF = K_OFF + D_GLA_K
G_OFF = V_OFF + D_GLA
MAIN_COLS = G_OFF + D_GLA

VMEM_LIMIT_BYTES = 62 * 1024 * 1024

FFN_TM = 1024
FFN_TF = 512
FFN_ROW_CHUNK = 256
FFN_HEAD_TF = 256
MIX_TM = 256
XATTN_TM = 512
KV_TN = 1024


def _rms(x, gain):
    return x * lax.rsqrt(jnp.mean(x * x, axis=-1, keepdims=True) + RMS_EPS) * gain


def _dot(a, b):
    return jnp.dot(a, b, preferred_element_type=F32)


def _resident(shape):
    return pl.BlockSpec(shape, lambda *_: (0,) * len(shape), pipeline_mode=pl.Buffered(1))


def _ffn_kernel(x_ref, gain_ref, wg_ref, wu_ref, wd_ref, fgain_ref, *rest, final_norm, n_side,
                emit_weights):
    n_w = 3 if emit_weights else 0
    side_in, o_ref, h_ref = rest[:n_side], rest[n_side], rest[-1]
    w_out = rest[n_side + 1:n_side + 1 + n_w]
    side_out = rest[n_side + 1 + n_w:2 * n_side + 1 + n_w]

    def weight(ref, k):
        if not emit_weights:
            return ref[...]
        w = ref[...].astype(BF16)
        w_out[k][...] = w
        return w

    j = pl.program_id(1)
    last = pl.num_programs(1) - 1

    def swiglu_part(h, with_side_casts):
        g = _dot(h, weight(wg_ref, 0))
        if with_side_casts:
            for src_ref, dst_ref in zip(side_in, side_out):
                dst_ref[...] = src_ref[...].astype(BF16)
        u = _dot(h, weight(wu_ref, 1))
        a = (g * jax.nn.sigmoid(g) * u).astype(BF16)
        return _dot(a, weight(wd_ref, 2))

    def edge_step(first):
        for c in range(x_ref.shape[0] // FFN_ROW_CHUNK):
            rows = slice(c * FFN_ROW_CHUNK, (c + 1) * FFN_ROW_CHUNK)
            if first:
                x = x_ref[rows, :]
                h = _rms(x, gain_ref[...]).astype(BF16)
                h_ref[rows, :] = h
                o_ref[rows, :] = 2.0 * x + swiglu_part(h, c == 0)
            else:
                y = 0.5 * (o_ref[rows, :] + swiglu_part(h_ref[rows, :], c == 0))
                if final_norm:
                    y = _rms(y, fgain_ref[...])
                o_ref[rows, :] = y

    @pl.when(j == 0)
    def _():
        edge_step(first=True)

    @pl.when(jnp.logical_and(j > 0, j < last))
    def _():
        o_ref[...] += swiglu_part(h_ref[...], True)

    @pl.when(j == last)
    def _():
        edge_step(first=False)


def _side_cast_block_rows(n_rows, n_steps):
    rows = BF16_SUBLANES
    while n_rows % rows or n_rows // rows > n_steps:
        rows += BF16_SUBLANES
    return rows


def _ffn(x, gain, wg, wu, wd, fgain, *, final_norm, name, tiles=None, tf=FFN_TF, side_casts=()):
    d = x.shape[1]
    dff = wg.shape[1]
    first, count = tiles if tiles is not None else (0, x.shape[0] // FFN_TM)
    t = count * FFN_TM
    emit_weights = wg.dtype == F32
    grid = (count, dff // tf)
    n_steps = grid[0] * grid[1]
    w_specs = [
        pl.BlockSpec((d, tf), lambda i, j: (0, j)),
        pl.BlockSpec((d, tf), lambda i, j: (0, j)),
        pl.BlockSpec((tf, d), lambda i, j: (j, 0)),
    ]

    def x_index(i, j):
        ahead = (j >= grid[1] // 2).astype(jnp.int32)
        return (first + jnp.minimum(i + ahead, count - 1), 0)

    w_out_specs = w_specs if emit_weights else []
    w_out_shapes = [jax.ShapeDtypeStruct(w.shape, BF16) for w in (wg, wu, wd)] if emit_weights else []
    side_specs_in, side_specs_out, side_shapes = [], [], []
    for w, n_rows, n_cols in side_casts:
        rows = _side_cast_block_rows(n_rows, n_steps)
        index_map = functools.partial(
            lambda i, j, n_blocks: (jnp.minimum(i * grid[1] + j, n_blocks - 1), 0),
            n_blocks=n_rows // rows)
        side_specs_in.append(pl.BlockSpec((rows, n_cols), index_map))
        side_specs_out.append(pl.BlockSpec((rows, n_cols), index_map))
        side_shapes.append(jax.ShapeDtypeStruct((n_rows, n_cols), BF16))
    outs = pl.pallas_call(
        functools.partial(_ffn_kernel, final_norm=final_norm, n_side=len(side_casts),
                          emit_weights=emit_weights),
        grid=grid,
        in_specs=[
            pl.BlockSpec((FFN_TM, d), x_index),
            pl.BlockSpec((1, d), lambda i, j: (0, 0)),
            *w_specs,
            pl.BlockSpec((1, d), lambda i, j: (0, 0)),
        ] + side_specs_in,
        out_specs=[pl.BlockSpec((FFN_TM, d), lambda i, j: (i, 0))] + w_out_specs + side_specs_out,
        out_shape=[jax.ShapeDtypeStruct((t, d), F32)] + w_out_shapes + side_shapes,
        scratch_shapes=[pltpu.VMEM((FFN_TM, d), BF16)],
        compiler_params=pltpu.CompilerParams(
            dimension_semantics=("arbitrary", "arbitrary"),
            vmem_limit_bytes=VMEM_LIMIT_BYTES),
        name=name,
    )(x, gain, wg, wu, wd, fgain, *[w for w, _, _ in side_casts])
    n_w = len(w_out_shapes)
    return outs[0], tuple(outs[1:1 + n_w]), tuple(outs[1 + n_w:])


def _mix_kernel(xh_ref, xt_ref, gain_ref, win_ref, wa_ref, wa2_ref, ba_ref, poolw_ref, pscale_ref,
                hnorm_ref, wout_ref, o_ref,
                proj_ref, oacc_ref, ubuf_ref, cat_ref, state_ref, *, n_head_tiles):
    tm = o_ref.shape[0]
    s = pl.program_id(1)
    in_head = pl.program_id(0) * pl.num_programs(1) + s < n_head_tiles

    def x_tile():
        return jnp.where(in_head, xh_ref[...], xt_ref[...])

    @pl.when(s == 0)
    def _():
        state_ref[...] = jnp.zeros_like(state_ref)
        ubuf_ref[0:POOL_HALO, :] = jnp.zeros((POOL_HALO, D_POOL), F32)

    h = _rms(x_tile(), gain_ref[...]).astype(BF16)

    def project(lo, width):
        proj_ref[:, lo:lo + width] = lax.dot_general(
            h, win_ref[lo:lo + width, :], (((1,), (1,)), ((), ())), preferred_element_type=F32)

    a_lr = _dot(h, wa_ref[...])
    project(0, D_POOL)
    z = _dot(a_lr.astype(BF16), wa2_ref[...]) + ba_ref[...]
    project(K_OFF, D_GLA_K)
    la = (jnp.minimum(z, 0.0) - jnp.log1p(jnp.exp(-jnp.abs(z)))) * (1.0 / GLA_GATE_TEMP)
    row = lax.broadcasted_iota(jnp.int32, (tm, tm), 0)
    col = lax.broadcasted_iota(jnp.int32, (tm, tm), 1)
    same_chunk = (row // CHUNK) == (col // CHUNK)
    chunk_sum = same_chunk.astype(BF16)
    chunk_prefix = (same_chunk & (row >= col)).astype(BF16)
    la_hi = la.astype(BF16)
    la_lo = (la - la_hi.astype(F32)).astype(BF16)
    cum = _dot(chunk_prefix, la_hi) + _dot(chunk_prefix, la_lo)
    b_end = _dot(chunk_sum, la_hi) + _dot(chunk_sum, la_lo)
    project(V_OFF, D_GLA)

    ubuf_ref[POOL_HALO:POOL_HALO + tm, :] = proj_ref[:, 0:D_POOL]
    t_idx = s * tm + lax.broadcasted_iota(jnp.int32, (tm, 1), 0)
    count = (t_idx + 1).astype(F32)
    for gi, w in enumerate(POOL_WINDOWS):
        cols = slice(gi * POOL_GROUP, (gi + 1) * POOL_GROUP)
        ext = ubuf_ref[:, cols]
        acc = ext
        shift = 1
        while shift < w:
            acc = acc + pltpu.roll(acc, shift, 0)
            shift *= 2
        win = acc[POOL_HALO:, :]
        d = win / jnp.minimum(count, float(w)) - ext[POOL_HALO:, :]
        y = _dot(d.astype(BF16), poolw_ref[gi]) * pscale_ref[:, cols]
        cat_ref[:, cols] = y.astype(BF16)
    ubuf_ref[0:POOL_HALO, :] = ubuf_ref[tm:tm + POOL_HALO, :]

    k_dec = (proj_ref[:, K_OFF:K_OFF + D_GLA_K] * jnp.exp(b_end - cum)).astype(BF16)
    decay = jnp.exp(b_end)
    n_chunks = tm // CHUNK
    heads_chunks = [(hd, c) for hd in range(GLA_HEADS) for c in range(n_chunks)]
    upd = {}
    for hd, c in heads_chunks:
        rows = slice(c * CHUNK, (c + 1) * CHUNK)
        v_h = proj_ref[rows, V_OFF + hd * GLA_DV:V_OFF + (hd + 1) * GLA_DV].astype(BF16)
        upd[hd, c] = lax.dot_general(v_h, k_dec[rows, hd * GLA_DK:(hd + 1) * GLA_DK],
                                     (((0,), (0,)), ((), ())),
                                     preferred_element_type=F32)
    project(Q_OFF, D_GLA_K)
    project(G_OFF, D_GLA)
    states = {}
    for hd in range(GLA_HEADS):
        st = state_ref[hd]
        for c in range(n_chunks):
            st = st * decay[c * CHUNK:c * CHUNK + 1, hd * GLA_DK:(hd + 1) * GLA_DK] + upd[hd, c]
            states[hd, c] = st.astype(BF16)
        state_ref[hd] = st
    q = (proj_ref[:, Q_OFF:Q_OFF + D_GLA_K] * (GLA_DK ** -0.5)).astype(BF16)
    for hd, c in heads_chunks:
        rows = slice(c * CHUNK, (c + 1) * CHUNK)
        oacc_ref[rows, hd * GLA_DV:(hd + 1) * GLA_DV] = lax.dot_general(
            q[rows, hd * GLA_DK:(hd + 1) * GLA_DK], states[hd, c], (((1,), (1,)), ((), ())),
            preferred_element_type=F32)
    o_ref[...] = x_tile() + _dot(cat_ref[:, 0:D_POOL], wout_ref[0:D_POOL, :])
    for hd in range(GLA_HEADS):
        hc = slice(hd * GLA_DV, (hd + 1) * GLA_DV)
        o = oacc_ref[:, hc]
        o = o * lax.rsqrt(jnp.mean(o * o, axis=-1, keepdims=True) + RMS_EPS) * hnorm_ref[:, hc]
        g_h = proj_ref[:, G_OFF + hd * GLA_DV:G_OFF + (hd + 1) * GLA_DV]
        y = o * (g_h * jax.nn.sigmoid(g_h))
        cat_ref[:, D_POOL + hd * GLA_DV:D_POOL + (hd + 1) * GLA_DV] = y.astype(BF16)

    o_ref[...] += _dot(cat_ref[:, D_POOL:], wout_ref[D_POOL:, :])


def _mixer(x_head, x_tail, b, gain, win, wa, wa2, ba, poolw, pscale, hnorm, wout):
    d = x_head.shape[1]
    tm = MIX_TM
    n_head = x_head.shape[0] // tm
    s = (x_head.shape[0] + x_tail.shape[0]) // b
    n_s = s // tm
    return pl.pallas_call(
        functools.partial(_mix_kernel, n_head_tiles=n_head),
        grid=(b, n_s),
        in_specs=[
            pl.BlockSpec((tm, d), lambda i, j: (jnp.minimum(i * n_s + j, n_head - 1), 0)),
            pl.BlockSpec((tm, d), lambda i, j: (jnp.maximum(i * n_s + j - n_head, 0), 0)),
            _resident((1, d)),
            _resident((MAIN_COLS, d)),
            _resident((d, LANES)),
            _resident((LANES, D_GLA_K)),
            _resident((1, D_GLA_K)),
            _resident((len(POOL_WINDOWS), POOL_GROUP, POOL_GROUP)),
            _resident((1, D_POOL)),
            _resident((1, D_GLA)),
            _resident((d, d)),
        ],
        out_specs=pl.BlockSpec((None, tm, d), lambda i, j: (i, j, 0)),
        out_shape=jax.ShapeDtypeStruct((b, s, d), F32),
        scratch_shapes=[
            pltpu.VMEM((tm, MAIN_COLS), F32),
            pltpu.VMEM((tm, D_GLA), F32),
            pltpu.VMEM((POOL_HALO + tm, D_POOL), F32),
            pltpu.VMEM((tm, d), BF16),
            pltpu.VMEM((GLA_HEADS, GLA_DV, GLA_DK), F32),
        ],
        compiler_params=pltpu.CompilerParams(
            dimension_semantics=("arbitrary", "arbitrary"),
            vmem_limit_bytes=VMEM_LIMIT_BYTES),
        name="mixer",
    )(x_head, x_tail, gain, win, wa, wa2, ba, poolw, pscale, hnorm, wout)


def _kv_kernel(mem_ref, gain_ref, wkv_ref, o_ref):
    h = _rms(mem_ref[...], gain_ref[...]).astype(BF16)
    o_ref[...] = _dot(h, wkv_ref[...]).astype(BF16)


def _mem_kv(mem, gain, wkv):
    t, d = mem.shape
    n = wkv.shape[1]
    return pl.pallas_call(
        _kv_kernel,
        grid=(n // KV_TN,),
        in_specs=[
            pl.BlockSpec((t, d), lambda j: (0, 0)),
            pl.BlockSpec((1, d), lambda j: (0, 0)),
            pl.BlockSpec((d, KV_TN), lambda j: (0, j)),
        ],
        out_specs=pl.BlockSpec((t, KV_TN), lambda j: (0, j)),
        out_shape=jax.ShapeDtypeStruct((t, n), BF16),
        compiler_params=pltpu.CompilerParams(
            dimension_semantics=("arbitrary",),
            vmem_limit_bytes=VMEM_LIMIT_BYTES),
        name="mem_kv",
    )(mem, gain, wkv)


def _xattn_kernel(x_ref, gain_ref, wq_ref, kv_ref, wo_ref, o_ref, cat_ref):
    half = x_ref.shape[0] // 2
    rows = [slice(0, half), slice(half, 2 * half)]
    head_cols = [slice(hd * XATTN_HEAD_DIM, (hd + 1) * XATTN_HEAD_DIM) for hd in range(XATTN_HEADS)]

    def query(r):
        h = _rms(x_ref[r, :], gain_ref[...]).astype(BF16)
        return (_dot(h, wq_ref[...]) * (XATTN_HEAD_DIM ** -0.5)).astype(BF16)

    def scores(q):
        return [lax.dot_general(q[:, c], kv_ref[:, c], (((1,), (1,)), ((), ())),
                                preferred_element_type=F32) for c in head_cols]

    def attend(r, logits):
        for c, lg in zip(head_cols, logits):
            e = jnp.exp(lg - jnp.max(lg, axis=-1, keepdims=True))
            p = e / jnp.sum(e, axis=-1, keepdims=True)
            v_h = kv_ref[:, D_MODEL + c.start:D_MODEL + c.stop]
            cat_ref[r, c] = _dot(p.astype(BF16), v_h).astype(BF16)

    def project_out(r):
        o_ref[r, :] = x_ref[r, :] + _dot(cat_ref[r, :], wo_ref[...])

    logits0 = scores(query(rows[0]))
    q1 = query(rows[1])
    attend(rows[0], logits0)
    logits1 = scores(q1)
    project_out(rows[0])
    attend(rows[1], logits1)
    project_out(rows[1])


def _xattn(x, gain, wq, kv, wo):
    b, s, d = x.shape
    m = kv.shape[1]
    tm = XATTN_TM
    return pl.pallas_call(
        _xattn_kernel,
        grid=(b, s // tm),
        in_specs=[
            pl.BlockSpec((None, tm, d), lambda i, j: (i, j, 0)),
            _resident((1, d)),
            _resident((d, d)),
            pl.BlockSpec((None, m, 2 * d), lambda i, j: (i, 0, 0)),
            _resident((d, d)),
        ],
        out_specs=pl.BlockSpec((None, tm, d), lambda i, j: (i, j, 0)),
        out_shape=jax.ShapeDtypeStruct((b, s, d), F32),
        scratch_shapes=[pltpu.VMEM((tm, d), BF16)],
        compiler_params=pltpu.CompilerParams(
            dimension_semantics=("parallel", "parallel"),
            vmem_limit_bytes=VMEM_LIMIT_BYTES),
        name="xattn",
    )(x, gain, wq, kv, wo)


def _layer(x, mem, ffn1_norm, ffn1_w_gate, ffn1_w_up, ffn1_w_down, mix_norm, w_in, pool_w,
           pool_scale, gla_w_a2, gla_b_a, gla_head_norm, w_out, xattn_norm, mem_norm,
           xattn_w_q, xattn_w_kv, xattn_w_o, ffn2_norm, ffn2_w_gate, ffn2_w_up, ffn2_w_down,
           final_gain, *, final_norm):
    b, s, d = x.shape
    row = lambda v: v.reshape(1, -1)
    bf = lambda w: w.astype(BF16)

    x = x.reshape(b * s, d)
    n_tiles = x.shape[0] // FFN_TM

    x_head, w1, _ = _ffn(x, row(ffn1_norm), ffn1_w_gate, ffn1_w_up, ffn1_w_down, row(final_gain),
                         final_norm=False, name="ffn_head", tiles=(0, 1), tf=FFN_HEAD_TF)
    w_in_t = w_in.T
    later_weights = (ffn2_w_gate, ffn2_w_up, ffn2_w_down, w_out, xattn_w_q, xattn_w_kv, xattn_w_o)
    side_casts = tuple((w, *w.shape) for w in later_weights) + ((w_in_t, MAIN_COLS, d),)
    x_tail, _, (w2g, w2u, w2d, w_o_mix, w_q, w_kv, w_o_att, w_main) = _ffn(
        x, row(ffn1_norm), *w1, row(final_gain), final_norm=False, name="ffn",
        tiles=(1, n_tiles - 1), side_casts=side_casts)

    w_a = jnp.pad(lax.optimization_barrier(w_in_t[MAIN_COLS:, :]).T,
                  ((0, 0), (0, LANES - GLA_GATE_RANK)))
    w_a2 = jnp.pad(gla_w_a2, ((0, LANES - GLA_GATE_RANK), (0, 0)))
    x = _mixer(x_head, x_tail, b, row(mix_norm), w_main, bf(w_a), bf(w_a2), row(gla_b_a),
               bf(pool_w), row(pool_scale), row(gla_head_norm), w_o_mix)

    mb, ml, _ = mem.shape
    kv = _mem_kv(mem.reshape(mb * ml, d), row(mem_norm), w_kv).reshape(mb, ml, 2 * d)
    x = _xattn(x, row(xattn_norm), w_q, kv, w_o_att)

    x, _, _ = _ffn(x.reshape(b * s, d), row(ffn2_norm), w2g, w2u, w2d, row(final_gain),
                   final_norm=final_norm, name="ffn_final" if final_norm else "ffn_tail")
    return x.reshape(b, s, d)


def kernel(x, mem, ffn1_norm, ffn1_w_gate, ffn1_w_up, ffn1_w_down, mix_norm, w_in, pool_w, pool_scale, gla_w_a2, gla_b_a, gla_head_norm, w_out, xattn_norm, mem_norm, xattn_w_q, xattn_w_kv, xattn_w_o, ffn2_norm, ffn2_w_gate, ffn2_w_up, ffn2_w_down, final_norm):
    depth = ffn1_norm.shape[0]
    for l in range(depth):
        x = _layer(x, mem, ffn1_norm[l], ffn1_w_gate[l], ffn1_w_up[l], ffn1_w_down[l], mix_norm[l],
                   w_in[l], pool_w[l], pool_scale[l], gla_w_a2[l], gla_b_a[l], gla_head_norm[l],
                   w_out[l], xattn_norm[l], mem_norm[l], xattn_w_q[l], xattn_w_kv[l], xattn_w_o[l],
                   ffn2_norm[l], ffn2_w_gate[l], ffn2_w_up[l], ffn2_w_down[l], final_norm,
                   final_norm=(l == depth - 1))
    return x
```

```python
import functools

import jax
import jax.numpy as jnp
from jax import lax
from jax.experimental import pallas as pl
from jax.experimental.pallas import tpu as pltpu

F32 = jnp.float32
BF16 = jnp.bfloat16

D_MODEL = 2048
CHUNK = 64
D_POOL = 1024
POOL_WINDOWS = (2, 4, 8, 16)
POOL_GROUP = 256
POOL_HALO = 16
D_GLA = 1024
GLA_HEADS = 4
GLA_DV = 256
GLA_DK = 128
D_GLA_K = GLA_HEADS * GLA_DK
GLA_GATE_RANK = 16
GLA_GATE_TEMP = 16.0
XATTN_HEADS = 4
XATTN_HEAD_DIM = D_MODEL // XATTN_HEADS
RMS_EPS = 1e-6
LANES = 128
BF16_SUBLANES = 16

Q_OFF = D_POOL
K_OFF = Q_OFF + D_GLA_K
V_OFF = K_OFF + D_GLA_K
G_OFF = V_OFF + D_GLA
MAIN_COLS = G_OFF + D_GLA

VMEM_LIMIT_BYTES = 62 * 1024 * 1024

FFN_TM = 1024
FFN_TF = 512
FFN_ROW_CHUNK = 256
FFN_HEAD_TF = 256
MIX_TM = 256
XATTN_TM = 512
KV_TN = 1024


def _rms(x, gain):
    return x * lax.rsqrt(jnp.mean(x * x, axis=-1, keepdims=True) + RMS_EPS) * gain


def _dot(a, b):
    return jnp.dot(a, b, preferred_element_type=F32)


def _resident(shape):
    return pl.BlockSpec(shape, lambda *_: (0,) * len(shape), pipeline_mode=pl.Buffered(1))


def _ffn_kernel(x_ref, gain_ref, wg_ref, wu_ref, wd_ref, fgain_ref, *rest, final_norm, n_side,
                emit_weights):
    n_w = 3 if emit_weights else 0
    side_in, o_ref, h_ref = rest[:n_side], rest[n_side], rest[-1]
    w_out = rest[n_side + 1:n_side + 1 + n_w]
    side_out = rest[n_side + 1 + n_w:2 * n_side + 1 + n_w]

    def weight(ref, k):
        if not emit_weights:
            return ref[...]
        w = ref[...].astype(BF16)
        w_out[k][...] = w
        return w

    j = pl.program_id(1)
    last = pl.num_programs(1) - 1

    def swiglu_part(h, with_side_casts):
        g = _dot(h, weight(wg_ref, 0))
        if with_side_casts:
            for src_ref, dst_ref in zip(side_in, side_out):
                dst_ref[...] = src_ref[...].astype(BF16)
        u = _dot(h, weight(wu_ref, 1))
        a = (g * jax.nn.sigmoid(g) * u).astype(BF16)
        return _dot(a, weight(wd_ref, 2))

    def edge_step(first):
        for c in range(x_ref.shape[0] // FFN_ROW_CHUNK):
            rows = slice(c * FFN_ROW_CHUNK, (c + 1) * FFN_ROW_CHUNK)
            if first:
                x = x_ref[rows, :]
                h = _rms(x, gain_ref[...]).astype(BF16)
                h_ref[rows, :] = h
                o_ref[rows, :] = 2.0 * x + swiglu_part(h, c == 0)
            else:
                y = 0.5 * (o_ref[rows, :] + swiglu_part(h_ref[rows, :], c == 0))
                if final_norm:
                    y = _rms(y, fgain_ref[...])
                o_ref[rows, :] = y

    @pl.when(j == 0)
    def _():
        edge_step(first=True)

    @pl.when(jnp.logical_and(j > 0, j < last))
    def _():
        o_ref[...] += swiglu_part(h_ref[...], True)

    @pl.when(j == last)
    def _():
        edge_step(first=False)


def _side_cast_block_rows(n_rows, n_steps):
    rows = BF16_SUBLANES
    while n_rows % rows or n_rows // rows > n_steps:
        rows += BF16_SUBLANES
    return rows


def _ffn(x, gain, wg, wu, wd, fgain, *, final_norm, name, tiles=None, tf=FFN_TF, side_casts=()):
    d = x.shape[1]
    dff = wg.shape[1]
    first, count = tiles if tiles is not None else (0, x.shape[0] // FFN_TM)
    t = count * FFN_TM
    emit_weights = wg.dtype == F32
    grid = (count, dff // tf)
    n_steps = grid[0] * grid[1]
    w_specs = [
        pl.BlockSpec((d, tf), lambda i, j: (0, j)),
        pl.BlockSpec((d, tf), lambda i, j: (0, j)),
        pl.BlockSpec((tf, d), lambda i, j: (j, 0)),
    ]

    def x_index(i, j):
        ahead = (j >= grid[1] // 2).astype(jnp.int32)
        return (first + jnp.minimum(i + ahead, count - 1), 0)

    w_out_specs = w_specs if emit_weights else []
    w_out_shapes = [jax.ShapeDtypeStruct(w.shape, BF16) for w in (wg, wu, wd)] if emit_weights else []
    side_specs_in, side_specs_out, side_shapes = [], [], []
    for w, row0, n_rows, n_cols in side_casts:
        rows = _side_cast_block_rows(n_rows, n_steps)
        assert row0 % rows == 0

        def block(i, j, n_blocks=n_rows // rows):
            return jnp.minimum(i * grid[1] + j, n_blocks - 1)

        side_specs_in.append(pl.BlockSpec(
            (rows, n_cols), lambda i, j, block=block, first_block=row0 // rows: (first_block + block(i, j), 0)))
        side_specs_out.append(pl.BlockSpec((rows, n_cols), lambda i, j, block=block: (block(i, j), 0)))
        side_shapes.append(jax.ShapeDtypeStruct((n_rows, n_cols), BF16))
    outs = pl.pallas_call(
        functools.partial(_ffn_kernel, final_norm=final_norm, n_side=len(side_casts),
                          emit_weights=emit_weights),
        grid=grid,
        in_specs=[
            pl.BlockSpec((FFN_TM, d), x_index),
            pl.BlockSpec((1, d), lambda i, j: (0, 0)),
            *w_specs,
            pl.BlockSpec((1, d), lambda i, j: (0, 0)),
        ] + side_specs_in,
        out_specs=[pl.BlockSpec((FFN_TM, d), lambda i, j: (i, 0))] + w_out_specs + side_specs_out,
        out_shape=[jax.ShapeDtypeStruct((t, d), F32)] + w_out_shapes + side_shapes,
        scratch_shapes=[pltpu.VMEM((FFN_TM, d), BF16)],
        compiler_params=pltpu.CompilerParams(
            dimension_semantics=("arbitrary", "arbitrary"),
            vmem_limit_bytes=VMEM_LIMIT_BYTES),
        name=name,
    )(x, gain, wg, wu, wd, fgain, *[w for w, _, _, _ in side_casts])
    n_w = len(w_out_shapes)
    return outs[0], tuple(outs[1:1 + n_w]), tuple(outs[1 + n_w:])


def _mix_kernel(xh_ref, xt_ref, gain_ref, win_ref, wa_ref, wa2_ref, ba_ref, poolw_ref, pscale_ref,
                hnorm_ref, wout_ref, o_ref,
                proj_ref, oacc_ref, ubuf_ref, cat_ref, state_ref, wa_pad_ref, *, n_head_tiles):
    tm = o_ref.shape[0]
    s = pl.program_id(1)
    tile = pl.program_id(0) * pl.num_programs(1) + s
    in_head = tile < n_head_tiles

    def x_tile():
        return jnp.where(in_head, xh_ref[...], xt_ref[...])

    @pl.when(tile == 0)
    def _():
        wa_pad_ref[...] = jnp.zeros_like(wa_pad_ref)
        wa_pad_ref[0:GLA_GATE_RANK, :] = wa_ref[...]

    @pl.when(s == 0)
    def _():
        state_ref[...] = jnp.zeros_like(state_ref)
        ubuf_ref[0:POOL_HALO, :] = jnp.zeros((POOL_HALO, D_POOL), F32)

    h = _rms(x_tile(), gain_ref[...]).astype(BF16)

    def project(lo, width):
        proj_ref[:, lo:lo + width] = lax.dot_general(
            h, win_ref[lo:lo + width, :], (((1,), (1,)), ((), ())), preferred_element_type=F32)

    a_lr = lax.dot_general(h, wa_pad_ref[...], (((1,), (1,)), ((), ())),
                           preferred_element_type=F32)
    project(0, D_POOL)
    z = _dot(a_lr.astype(BF16), wa2_ref[...]) + ba_ref[...]
    project(K_OFF, D_GLA_K)
    la = (jnp.minimum(z, 0.0) - jnp.log1p(jnp.exp(-jnp.abs(z)))) * (1.0 / GLA_GATE_TEMP)
    row = lax.broadcasted_iota(jnp.int32, (tm, tm), 0)
    col = lax.broadcasted_iota(jnp.int32, (tm, tm), 1)
    same_chunk = (row // CHUNK) == (col // CHUNK)
    chunk_sum = same_chunk.astype(BF16)
    chunk_prefix = (same_chunk & (row >= col)).astype(BF16)
    la_hi = la.astype(BF16)
    la_lo = (la - la_hi.astype(F32)).astype(BF16)
    cum = _dot(chunk_prefix, la_hi) + _dot(chunk_prefix, la_lo)
    b_end = _dot(chunk_sum, la_hi) + _dot(chunk_sum, la_lo)
    project(V_OFF, D_GLA)

    ubuf_ref[POOL_HALO:POOL_HALO + tm, :] = proj_ref[:, 0:D_POOL]
    t_idx = s * tm + lax.broadcasted_iota(jnp.int32, (tm, 1), 0)
    count = (t_idx + 1).astype(F32)
    for gi, w in enumerate(POOL_WINDOWS):
        cols = slice(gi * POOL_GROUP, (gi + 1) * POOL_GROUP)
        ext = ubuf_ref[:, cols]
        acc = ext
        shift = 1
        while shift < w:
            acc = acc + pltpu.roll(acc, shift, 0)
            shift *= 2
        win = acc[POOL_HALO:, :]
        d = win / jnp.minimum(count, float(w)) - ext[POOL_HALO:, :]
        y = _dot(d.astype(BF16), poolw_ref[gi]) * pscale_ref[:, cols]
        cat_ref[:, cols] = y.astype(BF16)
    ubuf_ref[0:POOL_HALO, :] = ubuf_ref[tm:tm + POOL_HALO, :]

    k_dec = (proj_ref[:, K_OFF:K_OFF + D_GLA_K] * jnp.exp(b_end - cum)).astype(BF16)
    decay = jnp.exp(b_end)
    n_chunks = tm // CHUNK
    row_chunk = lax.broadcasted_iota(jnp.int32, (tm, GLA_DK), 0) // CHUNK

    def chunk_blocks(a):
        return jnp.concatenate(
            [jnp.where(row_chunk == c, a, jnp.zeros_like(a)) for c in range(n_chunks)], axis=1)

    upd = {}
    for hd in range(GLA_HEADS):
        v_h = proj_ref[:, V_OFF + hd * GLA_DV:V_OFF + (hd + 1) * GLA_DV].astype(BF16)
        upd[hd] = lax.dot_general(v_h, chunk_blocks(k_dec[:, hd * GLA_DK:(hd + 1) * GLA_DK]),
                                  (((0,), (0,)), ((), ())),
                                  preferred_element_type=F32)
    project(Q_OFF, D_GLA_K)
    project(G_OFF, D_GLA)
    states = {}
    for hd in range(GLA_HEADS):
        st = state_ref[hd]
        for c in range(n_chunks):
            st = (st * decay[c * CHUNK:c * CHUNK + 1, hd * GLA_DK:(hd + 1) * GLA_DK]
                  + upd[hd][:, c * GLA_DK:(c + 1) * GLA_DK])
            states[hd, c] = st.astype(BF16)
        state_ref[hd] = st
    q = (proj_ref[:, Q_OFF:Q_OFF + D_GLA_K] * (GLA_DK ** -0.5)).astype(BF16)
    for hd in range(GLA_HEADS):
        s_all = jnp.concatenate([states[hd, c] for c in range(n_chunks)], axis=1)
        oacc_ref[:, hd * GLA_DV:(hd + 1) * GLA_DV] = lax.dot_general(
            chunk_blocks(q[:, hd * GLA_DK:(hd + 1) * GLA_DK]), s_all, (((1,), (1,)), ((), ())),
            preferred_element_type=F32)
    o_ref[...] = x_tile() + _dot(cat_ref[:, 0:D_POOL], wout_ref[0:D_POOL, :])
    for hd in range(GLA_HEADS):
        hc = slice(hd * GLA_DV, (hd + 1) * GLA_DV)
        o = oacc_ref[:, hc]
        o = o * lax.rsqrt(jnp.mean(o * o, axis=-1, keepdims=True) + RMS_EPS) * hnorm_ref[:, hc]
        g_h = proj_ref[:, G_OFF + hd * GLA_DV:G_OFF + (hd + 1) * GLA_DV]
        y = o * (g_h * jax.nn.sigmoid(g_h))
        cat_ref[:, D_POOL + hd * GLA_DV:D_POOL + (hd + 1) * GLA_DV] = y.astype(BF16)

    o_ref[...] += _dot(cat_ref[:, D_POOL:], wout_ref[D_POOL:, :])


def _mixer(x_head, x_tail, b, gain, win, wa, wa2, ba, poolw, pscale, hnorm, wout):
    d = x_head.shape[1]
    tm = MIX_TM
    n_head = x_head.shape[0] // tm
    s = (x_head.shape[0] + x_tail.shape[0]) // b
    n_s = s // tm
    return pl.pallas_call(
        functools.partial(_mix_kernel, n_head_tiles=n_head),
        grid=(b, n_s),
        in_specs=[
            pl.BlockSpec((tm, d), lambda i, j: (jnp.minimum(i * n_s + j, n_head - 1), 0)),
            pl.BlockSpec((tm, d), lambda i, j: (jnp.maximum(i * n_s + j - n_head, 0), 0)),
            _resident((1, d)),
            _resident((MAIN_COLS, d)),
            _resident((GLA_GATE_RANK, d)),
            _resident((LANES, D_GLA_K)),
            _resident((1, D_GLA_K)),
            _resident((len(POOL_WINDOWS), POOL_GROUP, POOL_GROUP)),
            _resident((1, D_POOL)),
            _resident((1, D_GLA)),
            _resident((d, d)),
        ],
        out_specs=pl.BlockSpec((None, tm, d), lambda i, j: (i, j, 0)),
        out_shape=jax.ShapeDtypeStruct((b, s, d), F32),
        scratch_shapes=[
            pltpu.VMEM((tm, MAIN_COLS), F32),
            pltpu.VMEM((tm, D_GLA), F32),
            pltpu.VMEM((POOL_HALO + tm, D_POOL), F32),
            pltpu.VMEM((tm, d), BF16),
            pltpu.VMEM((GLA_HEADS, GLA_DV, GLA_DK), F32),
            pltpu.VMEM((LANES, d), BF16),
        ],
        compiler_params=pltpu.CompilerParams(
            dimension_semantics=("arbitrary", "arbitrary"),
            vmem_limit_bytes=VMEM_LIMIT_BYTES),
        name="mixer",
    )(x_head, x_tail, gain, win, wa, wa2, ba, poolw, pscale, hnorm, wout)


def _kv_kernel(mem_ref, gain_ref, wkv_ref, o_ref):
    h = _rms(mem_ref[...], gain_ref[...]).astype(BF16)
    o_ref[...] = _dot(h, wkv_ref[...]).astype(BF16)


def _mem_kv(mem, gain, wkv):
    t, d = mem.shape
    n = wkv.shape[1]
    return pl.pallas_call(
        _kv_kernel,
        grid=(n // KV_TN,),
        in_specs=[
            pl.BlockSpec((t, d), lambda j: (0, 0)),
            pl.BlockSpec((1, d), lambda j: (0, 0)),
            pl.BlockSpec((d, KV_TN), lambda j: (0, j)),
        ],
        out_specs=pl.BlockSpec((t, KV_TN), lambda j: (0, j)),
        out_shape=jax.ShapeDtypeStruct((t, n), BF16),
        compiler_params=pltpu.CompilerParams(
            dimension_semantics=("arbitrary",),
            vmem_limit_bytes=VMEM_LIMIT_BYTES),
        name="mem_kv",
    )(mem, gain, wkv)


def _xattn_kernel(x_ref, gain_ref, wq_ref, kv_ref, wo_ref, o_ref, cat_ref):
    half = x_ref.shape[0] // 2
    rows = [slice(0, half), slice(half, 2 * half)]
    head_cols = [slice(hd * XATTN_HEAD_DIM, (hd + 1) * XATTN_HEAD_DIM) for hd in range(XATTN_HEADS)]

    def query(r):
        h = _rms(x_ref[r, :], gain_ref[...]).astype(BF16)
        return (_dot(h, wq_ref[...]) * (XATTN_HEAD_DIM ** -0.5)).astype(BF16)

    def scores(q):
        return [lax.dot_general(q[:, c], kv_ref[:, c], (((1,), (1,)), ((), ())),
                                preferred_element_type=F32) for c in head_cols]

    def attend(r, logits):
        for c, lg in zip(head_cols, logits):
            e = jnp.exp(lg - jnp.max(lg, axis=-1, keepdims=True))
            p = e / jnp.sum(e, axis=-1, keepdims=True)
            v_h = kv_ref[:, D_MODEL + c.start:D_MODEL + c.stop]
            cat_ref[r, c] = _dot(p.astype(BF16), v_h).astype(BF16)

    def project_out(r):
        o_ref[r, :] = x_ref[r, :] + _dot(cat_ref[r, :], wo_ref[...])

    logits0 = scores(query(rows[0]))
    q1 = query(rows[1])
    attend(rows[0], logits0)
    logits1 = scores(q1)
    project_out(rows[0])
    attend(rows[1], logits1)
    project_out(rows[1])


def _xattn(x, gain, wq, kv, wo):
    b, s, d = x.shape
    m = kv.shape[1]
    tm = XATTN_TM
    return pl.pallas_call(
        _xattn_kernel,
        grid=(b, s // tm),
        in_specs=[
            pl.BlockSpec((None, tm, d), lambda i, j: (i, j, 0)),
            _resident((1, d)),
            _resident((d, d)),
            pl.BlockSpec((None, m, 2 * d), lambda i, j: (i, 0, 0)),
            _resident((d, d)),
        ],
        out_specs=pl.BlockSpec((None, tm, d), lambda i, j: (i, j, 0)),
        out_shape=jax.ShapeDtypeStruct((b, s, d), F32),
        scratch_shapes=[pltpu.VMEM((tm, d), BF16)],
        compiler_params=pltpu.CompilerParams(
            dimension_semantics=("parallel", "parallel"),
            vmem_limit_bytes=VMEM_LIMIT_BYTES),
        name="xattn",
    )(x, gain, wq, kv, wo)


def _layer(x, mem, ffn1_norm, ffn1_w_gate, ffn1_w_up, ffn1_w_down, mix_norm, w_in, pool_w,
           pool_scale, gla_w_a2, gla_b_a, gla_head_norm, w_out, xattn_norm, mem_norm,
           xattn_w_q, xattn_w_kv, xattn_w_o, ffn2_norm, ffn2_w_gate, ffn2_w_up, ffn2_w_down,
           final_gain, *, final_norm):
    b, s, d = x.shape
    row = lambda v: v.reshape(1, -1)
    bf = lambda w: w.astype(BF16)

    x = x.reshape(b * s, d)
    n_tiles = x.shape[0] // FFN_TM

    x_head, w1, _ = _ffn(x, row(ffn1_norm), ffn1_w_gate, ffn1_w_up, ffn1_w_down, row(final_gain),
                         final_norm=False, name="ffn_head", tiles=(0, 1), tf=FFN_HEAD_TF)
    w_in_t = w_in.T
    later_weights = (ffn2_w_gate, ffn2_w_up, ffn2_w_down, w_out, xattn_w_q, xattn_w_kv, xattn_w_o)
    side_casts = tuple((w, 0, *w.shape) for w in later_weights) + (
        (w_in_t, 0, MAIN_COLS, d), (w_in_t, MAIN_COLS, GLA_GATE_RANK, d))
    x_tail, _, (w2g, w2u, w2d, w_o_mix, w_q, w_kv, w_o_att, w_main, w_a) = _ffn(
        x, row(ffn1_norm), *w1, row(final_gain), final_norm=False, name="ffn",
        tiles=(1, n_tiles - 1), side_casts=side_casts)

    w_a2 = jnp.pad(gla_w_a2, ((0, LANES - GLA_GATE_RANK), (0, 0)))
    x = _mixer(x_head, x_tail, b, row(mix_norm), w_main, w_a, bf(w_a2), row(gla_b_a),
               bf(pool_w), row(pool_scale), row(gla_head_norm), w_o_mix)

    mb, ml, _ = mem.shape
    kv = _mem_kv(mem.reshape(mb * ml, d), row(mem_norm), w_kv).reshape(mb, ml, 2 * d)
    x = _xattn(x, row(xattn_norm), w_q, kv, w_o_att)

    x, _, _ = _ffn(x.reshape(b * s, d), row(ffn2_norm), w2g, w2u, w2d, row(final_gain),
                   final_norm=final_norm, name="ffn_final" if final_norm else "ffn_tail")
    return x.reshape(b, s, d)


def kernel(x, mem, ffn1_norm, ffn1_w_gate, ffn1_w_up, ffn1_w_down, mix_norm, w_in, pool_w, pool_scale, gla_w_a2, gla_b_a, gla_head_norm, w_out, xattn_norm, mem_norm, xattn_w_q, xattn_w_kv, xattn_w_o, ffn2_norm, ffn2_w_gate, ffn2_w_up, ffn2_w_down, final_norm):
    depth = ffn1_norm.shape[0]
    for l in range(depth):
        x = _layer(x, mem, ffn1_norm[l], ffn1_w_gate[l], ffn1_w_up[l], ffn1_w_down[l], mix_norm[l],
                   w_in[l], pool_w[l], pool_scale[l], gla_w_a2[l], gla_b_a[l], gla_head_norm[l],
                   w_out[l], xattn_norm[l], mem_norm[l], xattn_w_q[l], xattn_w_kv[l], xattn_w_o[l],
                   ffn2_norm[l], ffn2_w_gate[l], ffn2_w_up[l], ffn2_w_down[l], final_norm,
                   final_norm=(l == depth - 1))
    return x
```

```python
import functools

import jax
import jax.numpy as jnp
from jax import lax
from jax.experimental import pallas as pl
from jax.experimental.pallas import tpu as pltpu

F32 = jnp.float32
BF16 = jnp.bfloat16

D_MODEL = 2048
CHUNK = 64
D_POOL = 1024
POOL_WINDOWS = (2, 4, 8, 16)
POOL_GROUP = 256
POOL_HALO = 16
D_GLA = 1024
GLA_HEADS = 4
GLA_DV = 256
GLA_DK = 128
D_GLA_K = GLA_HEADS * GLA_DK
GLA_GATE_RANK = 16
GLA_GATE_TEMP = 16.0
XATTN_HEADS = 4
XATTN_HEAD_DIM = D_MODEL // XATTN_HEADS
RMS_EPS = 1e-6
LANES = 128
BF16_SUBLANES = 16

Q_OFF = D_POOL
K_OFF = Q_OFF + D_GLA_K
V_OFF = K_OFF + D_GLA_K
G_OFF = V_OFF + D_GLA
MAIN_COLS = G_OFF + D_GLA

VMEM_LIMIT_BYTES = 62 * 1024 * 1024

FFN_TM = 1024
FFN_TF = 512
FFN_ROW_CHUNK = 256
FFN_HEAD_TF = 256
MIX_TM = 256
XATTN_TM = 512
KV_TN = 1024


def _rms(x, gain):
    return x * lax.rsqrt(jnp.mean(x * x, axis=-1, keepdims=True) + RMS_EPS) * gain


def _dot(a, b):
    return jnp.dot(a, b, preferred_element_type=F32)


def _resident(shape):
    return pl.BlockSpec(shape, lambda *_: (0,) * len(shape), pipeline_mode=pl.Buffered(1))


def _ffn_kernel(x_ref, gain_ref, wg_ref, wu_ref, wd_ref, fgain_ref, *rest, final_norm, n_side,
                emit_weights):
    n_w = 3 if emit_weights else 0
    side_in, o_ref, h_ref = rest[:n_side], rest[n_side], rest[-1]
    w_out = rest[n_side + 1:n_side + 1 + n_w]
    side_out = rest[n_side + 1 + n_w:2 * n_side + 1 + n_w]

    def weight(ref, k):
        if not emit_weights:
            return ref[...]
        w = ref[...].astype(BF16)
        w_out[k][...] = w
        return w

    j = pl.program_id(1)
    last = pl.num_programs(1) - 1

    def swiglu_part(h, with_side_casts):
        g = _dot(h, weight(wg_ref, 0))
        if with_side_casts:
            for src_ref, dst_ref in zip(side_in, side_out):
                dst_ref[...] = src_ref[...].astype(BF16)
        u = _dot(h, weight(wu_ref, 1))
        a = (g * jax.nn.sigmoid(g) * u).astype(BF16)
        return _dot(a, weight(wd_ref, 2))

    def edge_step(first):
        for c in range(x_ref.shape[0] // FFN_ROW_CHUNK):
            rows = slice(c * FFN_ROW_CHUNK, (c + 1) * FFN_ROW_CHUNK)
            if first:
                x = x_ref[rows, :]
                h = _rms(x, gain_ref[...]).astype(BF16)
                h_ref[rows, :] = h
                o_ref[rows, :] = 2.0 * x + swiglu_part(h, c == 0)
            else:
                y = 0.5 * (o_ref[rows, :] + swiglu_part(h_ref[rows, :], c == 0))
                if final_norm:
                    y = _rms(y, fgain_ref[...])
                o_ref[rows, :] = y

    @pl.when(j == 0)
    def _():
        edge_step(first=True)

    @pl.when(jnp.logical_and(j > 0, j < last))
    def _():
        o_ref[...] += swiglu_part(h_ref[...], True)

    @pl.when(j == last)
    def _():
        edge_step(first=False)


def _side_cast_block_rows(n_rows, n_steps):
    rows = BF16_SUBLANES
    while n_rows % rows or n_rows // rows > n_steps:
        rows += BF16_SUBLANES
    return rows


def _ffn(x, gain, wg, wu, wd, fgain, *, final_norm, name, tiles=None, tf=FFN_TF, side_casts=()):
    d = x.shape[1]
    dff = wg.shape[1]
    first, count = tiles if tiles is not None else (0, x.shape[0] // FFN_TM)
    t = count * FFN_TM
    emit_weights = wg.dtype == F32
    grid = (count, dff // tf)
    n_steps = grid[0] * grid[1]
    w_specs = [
        pl.BlockSpec((d, tf), lambda i, j: (0, j)),
        pl.BlockSpec((d, tf), lambda i, j: (0, j)),
        pl.BlockSpec((tf, d), lambda i, j: (j, 0)),
    ]

    def x_index(i, j):
        ahead = (j >= grid[1] // 2).astype(jnp.int32)
        return (first + jnp.minimum(i + ahead, count - 1), 0)

    w_out_specs = w_specs if emit_weights else []
    w_out_shapes = [jax.ShapeDtypeStruct(w.shape, BF16) for w in (wg, wu, wd)] if emit_weights else []
    side_specs_in, side_specs_out, side_shapes = [], [], []
    for w, row0, n_rows, n_cols in side_casts:
        rows = _side_cast_block_rows(n_rows, n_steps)
        assert row0 % rows == 0

        def block(i, j, n_blocks=n_rows // rows):
            return jnp.minimum(i * grid[1] + j, n_blocks - 1)

        side_specs_in.append(pl.BlockSpec(
            (rows, n_cols), lambda i, j, block=block, first_block=row0 // rows: (first_block + block(i, j), 0)))
        side_specs_out.append(pl.BlockSpec((rows, n_cols), lambda i, j, block=block: (block(i, j), 0)))
        side_shapes.append(jax.ShapeDtypeStruct((n_rows, n_cols), BF16))
    outs = pl.pallas_call(
        functools.partial(_ffn_kernel, final_norm=final_norm, n_side=len(side_casts),
                          emit_weights=emit_weights),
        grid=grid,
        in_specs=[
            pl.BlockSpec((FFN_TM, d), x_index),
            pl.BlockSpec((1, d), lambda i, j: (0, 0)),
            *w_specs,
            pl.BlockSpec((1, d), lambda i, j: (0, 0)),
        ] + side_specs_in,
        out_specs=[pl.BlockSpec((FFN_TM, d), lambda i, j: (i, 0))] + w_out_specs + side_specs_out,
        out_shape=[jax.ShapeDtypeStruct((t, d), F32)] + w_out_shapes + side_shapes,
        scratch_shapes=[pltpu.VMEM((FFN_TM, d), BF16)],
        compiler_params=pltpu.CompilerParams(
            dimension_semantics=("arbitrary", "arbitrary"),
            vmem_limit_bytes=VMEM_LIMIT_BYTES),
        name=name,
    )(x, gain, wg, wu, wd, fgain, *[w for w, _, _, _ in side_casts])
    n_w = len(w_out_shapes)
    return outs[0], tuple(outs[1:1 + n_w]), tuple(outs[1 + n_w:])


def _mix_kernel(xh_ref, xt_ref, gain_ref, win_ref, wa_ref, wa2_ref, ba_ref, poolw_ref, pscale_ref,
                hnorm_ref, wout_ref, o_ref,
                proj_ref, oacc_ref, ubuf_ref, cat_ref, state_ref, wa_pad_ref, *, n_head_tiles):
    tm = o_ref.shape[0]
    s = pl.program_id(1)
    tile = pl.program_id(0) * pl.num_programs(1) + s
    in_head = tile < n_head_tiles

    def x_tile():
        return jnp.where(in_head, xh_ref[...], xt_ref[...])

    @pl.when(tile == 0)
    def _():
        wa_pad_ref[...] = jnp.zeros_like(wa_pad_ref)
        wa_pad_ref[0:GLA_GATE_RANK, :] = wa_ref[...]

    @pl.when(s == 0)
    def _():
        state_ref[...] = jnp.zeros_like(state_ref)
        ubuf_ref[0:POOL_HALO, :] = jnp.zeros((POOL_HALO, D_POOL), F32)

    h = _rms(x_tile(), gain_ref[...]).astype(BF16)

    def project(lo, width):
        proj_ref[:, lo:lo + width] = lax.dot_general(
            h, win_ref[lo:lo + width, :], (((1,), (1,)), ((), ())), preferred_element_type=F32)

    a_lr = lax.dot_general(h, wa_pad_ref[...], (((1,), (1,)), ((), ())),
                           preferred_element_type=F32)
    project(0, D_POOL)
    z = _dot(a_lr.astype(BF16), wa2_ref[...]) + ba_ref[...]
    project(K_OFF, D_GLA_K)
    la = (jnp.minimum(z, 0.0) - jnp.log1p(jnp.exp(-jnp.abs(z)))) * (1.0 / GLA_GATE_TEMP)
    row = lax.broadcasted_iota(jnp.int32, (tm, tm), 0)
    col = lax.broadcasted_iota(jnp.int32, (tm, tm), 1)
    same_chunk = (row // CHUNK) == (col // CHUNK)
    chunk_sum = same_chunk.astype(BF16)
    chunk_prefix = (same_chunk & (row >= col)).astype(BF16)
    la_hi = la.astype(BF16)
    la_lo = (la - la_hi.astype(F32)).astype(BF16)
    sums = _dot(jnp.concatenate([chunk_prefix, chunk_sum], axis=0),
                jnp.concatenate([la_hi, la_lo], axis=1))
    cum = sums[:tm, :D_GLA_K] + sums[:tm, D_GLA_K:]
    b_end = sums[tm:, :D_GLA_K] + sums[tm:, D_GLA_K:]
    project(V_OFF, D_GLA)

    ubuf_ref[POOL_HALO:POOL_HALO + tm, :] = proj_ref[:, 0:D_POOL]
    t_idx = s * tm + lax.broadcasted_iota(jnp.int32, (tm, 1), 0)
    count = (t_idx + 1).astype(F32)
    for gi, w in enumerate(POOL_WINDOWS):
        cols = slice(gi * POOL_GROUP, (gi + 1) * POOL_GROUP)
        ext = ubuf_ref[:, cols]
        acc = ext
        shift = 1
        while shift < w:
            acc = acc + pltpu.roll(acc, shift, 0)
            shift *= 2
        win = acc[POOL_HALO:, :]
        d = win / jnp.minimum(count, float(w)) - ext[POOL_HALO:, :]
        y = _dot(d.astype(BF16), poolw_ref[gi]) * pscale_ref[:, cols]
        cat_ref[:, cols] = y.astype(BF16)
    ubuf_ref[0:POOL_HALO, :] = ubuf_ref[tm:tm + POOL_HALO, :]

    k_dec = (proj_ref[:, K_OFF:K_OFF + D_GLA_K] * jnp.exp(b_end - cum)).astype(BF16)
    decay = jnp.exp(b_end)
    n_chunks = tm // CHUNK
    row_chunk = lax.broadcasted_iota(jnp.int32, (tm, GLA_DK), 0) // CHUNK

    def chunk_blocks(a):
        return jnp.concatenate(
            [jnp.where(row_chunk == c, a, jnp.zeros_like(a)) for c in range(n_chunks)], axis=1)

    upd = {}
    for hd in range(GLA_HEADS):
        v_h = proj_ref[:, V_OFF + hd * GLA_DV:V_OFF + (hd + 1) * GLA_DV].astype(BF16)
        upd[hd] = lax.dot_general(v_h, chunk_blocks(k_dec[:, hd * GLA_DK:(hd + 1) * GLA_DK]),
                                  (((0,), (0,)), ((), ())),
                                  preferred_element_type=F32)
    project(Q_OFF, D_GLA_K)
    project(G_OFF, D_GLA)
    states = {}
    for hd in range(GLA_HEADS):
        st = state_ref[hd]
        for c in range(n_chunks):
            st = (st * decay[c * CHUNK:c * CHUNK + 1, hd * GLA_DK:(hd + 1) * GLA_DK]
                  + upd[hd][:, c * GLA_DK:(c + 1) * GLA_DK])
            states[hd, c] = st.astype(BF16)
        state_ref[hd] = st
    q = (proj_ref[:, Q_OFF:Q_OFF + D_GLA_K] * (GLA_DK ** -0.5)).astype(BF16)
    for hd in range(GLA_HEADS):
        s_all = jnp.concatenate([states[hd, c] for c in range(n_chunks)], axis=1)
        oacc_ref[:, hd * GLA_DV:(hd + 1) * GLA_DV] = lax.dot_general(
            chunk_blocks(q[:, hd * GLA_DK:(hd + 1) * GLA_DK]), s_all, (((1,), (1,)), ((), ())),
            preferred_element_type=F32)
    o_ref[...] = x_tile() + _dot(cat_ref[:, 0:D_POOL], wout_ref[0:D_POOL, :])
    for hd in range(GLA_HEADS):
        hc = slice(hd * GLA_DV, (hd + 1) * GLA_DV)
        o = oacc_ref[:, hc]
        o = o * lax.rsqrt(jnp.mean(o * o, axis=-1, keepdims=True) + RMS_EPS) * hnorm_ref[:, hc]
        g_h = proj_ref[:, G_OFF + hd * GLA_DV:G_OFF + (hd + 1) * GLA_DV]
        y = o * (g_h * jax.nn.sigmoid(g_h))
        cat_ref[:, D_POOL + hd * GLA_DV:D_POOL + (hd + 1) * GLA_DV] = y.astype(BF16)

    o_ref[...] += _dot(cat_ref[:, D_POOL:], wout_ref[D_POOL:, :])


def _mixer(x_head, x_tail, b, gain, win, wa, wa2, ba, poolw, pscale, hnorm, wout):
    d = x_head.shape[1]
    tm = MIX_TM
    n_head = x_head.shape[0] // tm
    s = (x_head.shape[0] + x_tail.shape[0]) // b
    n_s = s // tm
    return pl.pallas_call(
        functools.partial(_mix_kernel, n_head_tiles=n_head),
        grid=(b, n_s),
        in_specs=[
            pl.BlockSpec((tm, d), lambda i, j: (jnp.minimum(i * n_s + j, n_head - 1), 0)),
            pl.BlockSpec((tm, d), lambda i, j: (jnp.maximum(i * n_s + j - n_head, 0), 0)),
            _resident((1, d)),
            _resident((MAIN_COLS, d)),
            _resident((GLA_GATE_RANK, d)),
            _resident((LANES, D_GLA_K)),
            _resident((1, D_GLA_K)),
            _resident((len(POOL_WINDOWS), POOL_GROUP, POOL_GROUP)),
            _resident((1, D_POOL)),
            _resident((1, D_GLA)),
            _resident((d, d)),
        ],
        out_specs=pl.BlockSpec((None, tm, d), lambda i, j: (i, j, 0)),
        out_shape=jax.ShapeDtypeStruct((b, s, d), F32),
        scratch_shapes=[
            pltpu.VMEM((tm, MAIN_COLS), F32),
            pltpu.VMEM((tm, D_GLA), F32),
            pltpu.VMEM((POOL_HALO + tm, D_POOL), F32),
            pltpu.VMEM((tm, d), BF16),
            pltpu.VMEM((GLA_HEADS, GLA_DV, GLA_DK), F32),
            pltpu.VMEM((LANES, d), BF16),
        ],
        compiler_params=pltpu.CompilerParams(
            dimension_semantics=("arbitrary", "arbitrary"),
            vmem_limit_bytes=VMEM_LIMIT_BYTES),
        name="mixer",
    )(x_head, x_tail, gain, win, wa, wa2, ba, poolw, pscale, hnorm, wout)


def _kv_kernel(mem_ref, gain_ref, wkv_ref, o_ref):
    h = _rms(mem_ref[...], gain_ref[...]).astype(BF16)
    o_ref[...] = _dot(h, wkv_ref[...]).astype(BF16)


def _mem_kv(mem, gain, wkv):
    t, d = mem.shape
    n = wkv.shape[1]
    return pl.pallas_call(
        _kv_kernel,
        grid=(n // KV_TN,),
        in_specs=[
            pl.BlockSpec((t, d), lambda j: (0, 0)),
            pl.BlockSpec((1, d), lambda j: (0, 0)),
            pl.BlockSpec((d, KV_TN), lambda j: (0, j)),
        ],
        out_specs=pl.BlockSpec((t, KV_TN), lambda j: (0, j)),
        out_shape=jax.ShapeDtypeStruct((t, n), BF16),
        compiler_params=pltpu.CompilerParams(
            dimension_semantics=("arbitrary",),
            vmem_limit_bytes=VMEM_LIMIT_BYTES),
        name="mem_kv",
    )(mem, gain, wkv)


def _xattn_kernel(x_ref, gain_ref, wq_ref, kv_ref, wo_ref, o_ref, cat_ref):
    half = x_ref.shape[0] // 2
    rows = [slice(0, half), slice(half, 2 * half)]
    head_cols = [slice(hd * XATTN_HEAD_DIM, (hd + 1) * XATTN_HEAD_DIM) for hd in range(XATTN_HEADS)]

    def query(r):
        h = _rms(x_ref[r, :], gain_ref[...]).astype(BF16)
        return (_dot(h, wq_ref[...]) * (XATTN_HEAD_DIM ** -0.5)).astype(BF16)

    def scores(q):
        return [lax.dot_general(q[:, c], kv_ref[:, c], (((1,), (1,)), ((), ())),
                                preferred_element_type=F32) for c in head_cols]

    def attend(r, logits):
        for c, lg in zip(head_cols, logits):
            e = jnp.exp(lg - jnp.max(lg, axis=-1, keepdims=True))
            p = e / jnp.sum(e, axis=-1, keepdims=True)
            v_h = kv_ref[:, D_MODEL + c.start:D_MODEL + c.stop]
            cat_ref[r, c] = _dot(p.astype(BF16), v_h).astype(BF16)

    def project_out(r):
        o_ref[r, :] = x_ref[r, :] + _dot(cat_ref[r, :], wo_ref[...])

    logits0 = scores(query(rows[0]))
    q1 = query(rows[1])
    attend(rows[0], logits0)
    logits1 = scores(q1)
    project_out(rows[0])
    attend(rows[1], logits1)
    project_out(rows[1])


def _xattn(x, gain, wq, kv, wo):
    b, s, d = x.shape
    m = kv.shape[1]
    tm = XATTN_TM
    return pl.pallas_call(
        _xattn_kernel,
        grid=(b, s // tm),
        in_specs=[
            pl.BlockSpec((None, tm, d), lambda i, j: (i, j, 0)),
            _resident((1, d)),
            _resident((d, d)),
            pl.BlockSpec((None, m, 2 * d), lambda i, j: (i, 0, 0)),
            _resident((d, d)),
        ],
        out_specs=pl.BlockSpec((None, tm, d), lambda i, j: (i, j, 0)),
        out_shape=jax.ShapeDtypeStruct((b, s, d), F32),
        scratch_shapes=[pltpu.VMEM((tm, d), BF16)],
        compiler_params=pltpu.CompilerParams(
            dimension_semantics=("parallel", "parallel"),
            vmem_limit_bytes=VMEM_LIMIT_BYTES),
        name="xattn",
    )(x, gain, wq, kv, wo)


def _layer(x, mem, ffn1_norm, ffn1_w_gate, ffn1_w_up, ffn1_w_down, mix_norm, w_in, pool_w,
           pool_scale, gla_w_a2, gla_b_a, gla_head_norm, w_out, xattn_norm, mem_norm,
           xattn_w_q, xattn_w_kv, xattn_w_o, ffn2_norm, ffn2_w_gate, ffn2_w_up, ffn2_w_down,
           final_gain, *, final_norm):
    b, s, d = x.shape
    row = lambda v: v.reshape(1, -1)
    bf = lambda w: w.astype(BF16)

    x = x.reshape(b * s, d)
    n_tiles = x.shape[0] // FFN_TM

    w_in_t = w_in.T
    x_head, w1, (w_a,) = _ffn(
        x, row(ffn1_norm), ffn1_w_gate, ffn1_w_up, ffn1_w_down, row(final_gain),
        final_norm=False, name="ffn_head", tiles=(0, 1), tf=FFN_HEAD_TF,
        side_casts=((w_in_t, MAIN_COLS, GLA_GATE_RANK, d),))
    later_weights = (ffn2_w_gate, ffn2_w_up, ffn2_w_down, w_out, xattn_w_q, xattn_w_kv, xattn_w_o)
    side_casts = tuple((w, 0, *w.shape) for w in later_weights) + ((w_in_t, 0, MAIN_COLS, d),)
    x_tail, _, (w2g, w2u, w2d, w_o_mix, w_q, w_kv, w_o_att, w_main) = _ffn(
        x, row(ffn1_norm), *w1, row(final_gain), final_norm=False, name="ffn",
        tiles=(1, n_tiles - 1), side_casts=side_casts)

    w_a2 = jnp.pad(gla_w_a2, ((0, LANES - GLA_GATE_RANK), (0, 0)))
    x = _mixer(x_head, x_tail, b, row(mix_norm), w_main, w_a, bf(w_a2), row(gla_b_a),
               bf(pool_w), row(pool_scale), row(gla_head_norm), w_o_mix)

    mb, ml, _ = mem.shape
    kv = _mem_kv(mem.reshape(mb * ml, d), row(mem_norm), w_kv).reshape(mb, ml, 2 * d)
    x = _xattn(x, row(xattn_norm), w_q, kv, w_o_att)

    x, _, _ = _ffn(x.reshape(b * s, d), row(ffn2_norm), w2g, w2u, w2d, row(final_gain),
                   final_norm=final_norm, name="ffn_final" if final_norm else "ffn_tail")
    return x.reshape(b, s, d)


def kernel(x, mem, ffn1_norm, ffn1_w_gate, ffn1_w_up, ffn1_w_down, mix_norm, w_in, pool_w, pool_scale, gla_w_a2, gla_b_a, gla_head_norm, w_out, xattn_norm, mem_norm, xattn_w_q, xattn_w_kv, xattn_w_o, ffn2_norm, ffn2_w_gate, ffn2_w_up, ffn2_w_down, final_norm):
    depth = ffn1_norm.shape[0]
    for l in range(depth):
        x = _layer(x, mem, ffn1_norm[l], ffn1_w_gate[l], ffn1_w_up[l], ffn1_w_down[l], mix_norm[l],
                   w_in[l], pool_w[l], pool_scale[l], gla_w_a2[l], gla_b_a[l], gla_head_norm[l],
                   w_out[l], xattn_norm[l], mem_norm[l], xattn_w_q[l], xattn_w_kv[l], xattn_w_o[l],
                   ffn2_norm[l], ffn2_w_gate[l], ffn2_w_up[l], ffn2_w_down[l], final_norm,
                   final_norm=(l == depth - 1))
    return x
```

```python
import functools

import jax
import jax.numpy as jnp
from jax import lax
from jax.experimental import pallas as pl
from jax.experimental.pallas import tpu as pltpu

F32 = jnp.float32
BF16 = jnp.bfloat16

D_MODEL = 2048
CHUNK = 64
D_POOL = 1024
POOL_WINDOWS = (2, 4, 8, 16)
POOL_GROUP = 256
POOL_HALO = 16
D_GLA = 1024
GLA_HEADS = 4
GLA_DV = 256
GLA_DK = 128
D_GLA_K = GLA_HEADS * GLA_DK
GLA_GATE_RANK = 16
GLA_GATE_TEMP = 16.0
XATTN_HEADS = 4
XATTN_HEAD_DIM = D_MODEL // XATTN_HEADS
RMS_EPS = 1e-6
LANES = 128
BF16_SUBLANES = 16

Q_OFF = D_POOL
K_OFF = Q_OFF + D_GLA_K
V_OFF = K_OFF + D_GLA_K
G_OFF = V_OFF + D_GLA
MAIN_COLS = G_OFF + D_GLA

VMEM_LIMIT_BYTES = 62 * 1024 * 1024

FFN_TM = 1024
FFN_TF = 512
FFN_ROW_CHUNK = 256
FFN_HEAD_TF = 256
MIX_TM = 256
XATTN_TM = 512
KV_TN = 1024


def _rms(x, gain):
    return x * lax.rsqrt(jnp.mean(x * x, axis=-1, keepdims=True) + RMS_EPS) * gain


def _dot(a, b):
    return jnp.dot(a, b, preferred_element_type=F32)


def _resident(shape):
    return pl.BlockSpec(shape, lambda *_: (0,) * len(shape), pipeline_mode=pl.Buffered(1))


def _ffn_kernel(x_ref, gain_ref, wg_ref, wu_ref, wd_ref, fgain_ref, *rest, final_norm, n_side,
                emit_weights):
    n_w = 3 if emit_weights else 0
    side_in, o_ref, h_ref = rest[:n_side], rest[n_side], rest[-1]
    w_out = rest[n_side + 1:n_side + 1 + n_w]
    side_out = rest[n_side + 1 + n_w:2 * n_side + 1 + n_w]

    def weights():
        blocks = [ref[...] for ref in (wg_ref, wu_ref, wd_ref)]
        if emit_weights:
            blocks = [w.astype(BF16) for w in blocks]
            for dst_ref, w in zip(w_out, blocks):
                dst_ref[...] = w
        return blocks

    j = pl.program_id(1)
    last = pl.num_programs(1) - 1

    def swiglu_part(h, wg, wu, wd, with_side_casts=False):
        g = _dot(h, wg)
        if with_side_casts:
            _side_cast(side_in, side_out)
        u = _dot(h, wu)
        a = (g * jax.nn.sigmoid(g) * u).astype(BF16)
        return _dot(a, wd)

    def edge_step(first):
        w = weights()
        for c in range(x_ref.shape[0] // FFN_ROW_CHUNK):
            rows = slice(c * FFN_ROW_CHUNK, (c + 1) * FFN_ROW_CHUNK)
            if first:
                h = _rms(x_ref[rows, :], gain_ref[...]).astype(BF16)
                h_ref[rows, :] = h
                o_ref[rows, :] = swiglu_part(h, *w, with_side_casts=c == 0)
            else:
                part = swiglu_part(h_ref[rows, :], *w, with_side_casts=c == 0)
                y = x_ref[rows, :] + 0.5 * (o_ref[rows, :] + part)
                if final_norm:
                    y = _rms(y, fgain_ref[...])
                o_ref[rows, :] = y

    @pl.when(j == 0)
    def _():
        edge_step(first=True)

    @pl.when(jnp.logical_and(j > 0, j < last))
    def _():
        o_ref[...] += swiglu_part(h_ref[...], *weights(), with_side_casts=True)

    @pl.when(j == last)
    def _():
        edge_step(first=False)


def _side_cast_block_rows(n_rows, n_steps):
    rows = BF16_SUBLANES
    while n_rows % rows or n_rows // rows > n_steps:
        rows += BF16_SUBLANES
    return rows


def _side_cast_specs(side_casts, grid):
    specs_in, specs_out, shapes = [], [], []
    for _, row0, n_rows, n_cols in side_casts:
        rows = _side_cast_block_rows(n_rows, grid[0] * grid[1])
        assert row0 % rows == 0

        def block(i, j, n_blocks=n_rows // rows):
            return jnp.minimum(i * grid[1] + j, n_blocks - 1)

        specs_in.append(pl.BlockSpec(
            (rows, n_cols), lambda i, j, block=block, first_block=row0 // rows: (first_block + block(i, j), 0)))
        specs_out.append(pl.BlockSpec((rows, n_cols), lambda i, j, block=block: (block(i, j), 0)))
        shapes.append(jax.ShapeDtypeStruct((n_rows, n_cols), BF16))
    return specs_in, specs_out, shapes


def _side_cast(src_refs, dst_refs):
    for src_ref, dst_ref in zip(src_refs, dst_refs):
        dst_ref[...] = src_ref[...].astype(BF16)


def _ffn(x, gain, wg, wu, wd, fgain, *, final_norm, name, tiles=None, tf=FFN_TF, side_casts=()):
    d = x.shape[1]
    dff = wg.shape[1]
    first, count = tiles if tiles is not None else (0, x.shape[0] // FFN_TM)
    t = count * FFN_TM
    emit_weights = wg.dtype == F32
    grid = (count, dff // tf)
    assert grid[1] >= 2
    w_specs = [
        pl.BlockSpec((d, tf), lambda i, j: (0, j)),
        pl.BlockSpec((d, tf), lambda i, j: (0, j)),
        pl.BlockSpec((tf, d), lambda i, j: (j, 0)),
    ]
    w_out_specs = w_specs if emit_weights else []
    w_out_shapes = [jax.ShapeDtypeStruct(w.shape, BF16) for w in (wg, wu, wd)] if emit_weights else []
    side_specs_in, side_specs_out, side_shapes = _side_cast_specs(side_casts, grid)
    outs = pl.pallas_call(
        functools.partial(_ffn_kernel, final_norm=final_norm, n_side=len(side_casts),
                          emit_weights=emit_weights),
        grid=grid,
        in_specs=[
            pl.BlockSpec((FFN_TM, d), lambda i, j: (first + i, 0)),
            pl.BlockSpec((1, d), lambda i, j: (0, 0)),
            *w_specs,
            pl.BlockSpec((1, d), lambda i, j: (0, 0)),
        ] + side_specs_in,
        out_specs=[pl.BlockSpec((FFN_TM, d), lambda i, j: (i, 0))] + w_out_specs + side_specs_out,
        out_shape=[jax.ShapeDtypeStruct((t, d), F32)] + w_out_shapes + side_shapes,
        scratch_shapes=[pltpu.VMEM((FFN_TM, d), BF16)],
        compiler_params=pltpu.CompilerParams(
            dimension_semantics=("arbitrary", "arbitrary"),
            vmem_limit_bytes=VMEM_LIMIT_BYTES),
        name=name,
    )(x, gain, wg, wu, wd, fgain, *[w for w, _, _, _ in side_casts])
    n_w = len(w_out_shapes)
    return outs[0], tuple(outs[1:1 + n_w]), tuple(outs[1 + n_w:])


def _mix_kernel(xh_ref, xt_ref, gain_ref, win_ref, wa_ref, wa2_ref, ba_ref, poolw_ref, pscale_ref,
                hnorm_ref, wout_ref, o_ref,
                proj_ref, oacc_ref, ubuf_ref, cat_ref, state_ref, wa_pad_ref, *, n_head_tiles):
    tm = o_ref.shape[0]
    s = pl.program_id(1)
    tile = pl.program_id(0) * pl.num_programs(1) + s
    in_head = tile < n_head_tiles

    def x_tile():
        return jnp.where(in_head, xh_ref[...], xt_ref[...])

    @pl.when(tile == 0)
    def _():
        wa_pad_ref[...] = jnp.zeros_like(wa_pad_ref)
        wa_pad_ref[0:GLA_GATE_RANK, :] = wa_ref[...]

    @pl.when(s == 0)
    def _():
        state_ref[...] = jnp.zeros_like(state_ref)
        ubuf_ref[0:POOL_HALO, :] = jnp.zeros((POOL_HALO, D_POOL), F32)

    h = _rms(x_tile(), gain_ref[...]).astype(BF16)

    def project(lo, width):
        proj_ref[:, lo:lo + width] = lax.dot_general(
            h, win_ref[lo:lo + width, :], (((1,), (1,)), ((), ())), preferred_element_type=F32)

    a_lr = lax.dot_general(h, wa_pad_ref[...], (((1,), (1,)), ((), ())),
                           preferred_element_type=F32)
    project(0, D_POOL)
    z = _dot(a_lr.astype(BF16), wa2_ref[...]) + ba_ref[...]
    project(K_OFF, D_GLA_K)
    la = (jnp.minimum(z, 0.0) - jnp.log1p(jnp.exp(-jnp.abs(z)))) * (1.0 / GLA_GATE_TEMP)
    row = lax.broadcasted_iota(jnp.int32, (tm, tm), 0)
    col = lax.broadcasted_iota(jnp.int32, (tm, tm), 1)
    same_chunk = (row // CHUNK) == (col // CHUNK)
    chunk_sum = same_chunk.astype(BF16)
    chunk_prefix = (same_chunk & (row >= col)).astype(BF16)
    la_hi = la.astype(BF16)
    la_lo = (la - la_hi.astype(F32)).astype(BF16)
    sums = _dot(jnp.concatenate([chunk_prefix, chunk_sum], axis=0),
                jnp.concatenate([la_hi, la_lo], axis=1))
    cum = sums[:tm, :D_GLA_K] + sums[:tm, D_GLA_K:]
    b_end = sums[tm:, :D_GLA_K] + sums[tm:, D_GLA_K:]
    project(V_OFF, D_GLA)

    ubuf_ref[POOL_HALO:POOL_HALO + tm, :] = proj_ref[:, 0:D_POOL]
    t_idx = s * tm + lax.broadcasted_iota(jnp.int32, (tm, 1), 0)
    count = (t_idx + 1).astype(F32)
    for gi, w in enumerate(POOL_WINDOWS):
        cols = slice(gi * POOL_GROUP, (gi + 1) * POOL_GROUP)
        ext = ubuf_ref[:, cols]
        acc = ext
        shift = 1
        while shift < w:
            acc = acc + pltpu.roll(acc, shift, 0)
            shift *= 2
        win = acc[POOL_HALO:, :]
        d = win / jnp.minimum(count, float(w)) - ext[POOL_HALO:, :]
        y = _dot(d.astype(BF16), poolw_ref[gi]) * pscale_ref[:, cols]
        cat_ref[:, cols] = y.astype(BF16)
    ubuf_ref[0:POOL_HALO, :] = ubuf_ref[tm:tm + POOL_HALO, :]

    k_dec = (proj_ref[:, K_OFF:K_OFF + D_GLA_K] * jnp.exp(b_end - cum)).astype(BF16)
    decay = jnp.exp(b_end)
    n_chunks = tm // CHUNK
    row_chunk = lax.broadcasted_iota(jnp.int32, (tm, GLA_DK), 0) // CHUNK

    def chunk_blocks(a):
        return jnp.concatenate(
            [jnp.where(row_chunk == c, a, jnp.zeros_like(a)) for c in range(n_chunks)], axis=1)

    upd = {}
    for hd in range(GLA_HEADS):
        v_h = proj_ref[:, V_OFF + hd * GLA_DV:V_OFF + (hd + 1) * GLA_DV].astype(BF16)
        upd[hd] = lax.dot_general(v_h, chunk_blocks(k_dec[:, hd * GLA_DK:(hd + 1) * GLA_DK]),
                                  (((0,), (0,)), ((), ())),
                                  preferred_element_type=F32)
    project(Q_OFF, D_GLA_K)
    project(G_OFF, D_GLA)
    states = {}
    for hd in range(GLA_HEADS):
        st = state_ref[hd]
        for c in range(n_chunks):
            st = (st * decay[c * CHUNK:c * CHUNK + 1, hd * GLA_DK:(hd + 1) * GLA_DK]
                  + upd[hd][:, c * GLA_DK:(c + 1) * GLA_DK])
            states[hd, c] = st.astype(BF16)
        state_ref[hd] = st
    q = (proj_ref[:, Q_OFF:Q_OFF + D_GLA_K] * (GLA_DK ** -0.5)).astype(BF16)
    for hd in range(GLA_HEADS):
        s_all = jnp.concatenate([states[hd, c] for c in range(n_chunks)], axis=1)
        oacc_ref[:, hd * GLA_DV:(hd + 1) * GLA_DV] = lax.dot_general(
            chunk_blocks(q[:, hd * GLA_DK:(hd + 1) * GLA_DK]), s_all, (((1,), (1,)), ((), ())),
            preferred_element_type=F32)
    o_ref[...] = x_tile() + _dot(cat_ref[:, 0:D_POOL], wout_ref[0:D_POOL, :])
    for hd in range(GLA_HEADS):
        hc = slice(hd * GLA_DV, (hd + 1) * GLA_DV)
        o = oacc_ref[:, hc]
        o = o * lax.rsqrt(jnp.mean(o * o, axis=-1, keepdims=True) + RMS_EPS) * hnorm_ref[:, hc]
        g_h = proj_ref[:, G_OFF + hd * GLA_DV:G_OFF + (hd + 1) * GLA_DV]
        y = o * (g_h * jax.nn.sigmoid(g_h))
        cat_ref[:, D_POOL + hd * GLA_DV:D_POOL + (hd + 1) * GLA_DV] = y.astype(BF16)

    o_ref[...] += _dot(cat_ref[:, D_POOL:], wout_ref[D_POOL:, :])


def _mixer(x_head, x_tail, b, gain, win, wa, wa2, ba, poolw, pscale, hnorm, wout):
    d = x_head.shape[1]
    tm = MIX_TM
    n_head = x_head.shape[0] // tm
    s = (x_head.shape[0] + x_tail.shape[0]) // b
    n_s = s // tm
    return pl.pallas_call(
        functools.partial(_mix_kernel, n_head_tiles=n_head),
        grid=(b, n_s),
        in_specs=[
            pl.BlockSpec((tm, d), lambda i, j: (jnp.minimum(i * n_s + j, n_head - 1), 0)),
            pl.BlockSpec((tm, d), lambda i, j: (jnp.maximum(i * n_s + j - n_head, 0), 0)),
            _resident((1, d)),
            _resident((MAIN_COLS, d)),
            _resident((GLA_GATE_RANK, d)),
            _resident((LANES, D_GLA_K)),
            _resident((1, D_GLA_K)),
            _resident((len(POOL_WINDOWS), POOL_GROUP, POOL_GROUP)),
            _resident((1, D_POOL)),
            _resident((1, D_GLA)),
            _resident((d, d)),
        ],
        out_specs=pl.BlockSpec((None, tm, d), lambda i, j: (i, j, 0)),
        out_shape=jax.ShapeDtypeStruct((b, s, d), F32),
        scratch_shapes=[
            pltpu.VMEM((tm, MAIN_COLS), F32),
            pltpu.VMEM((tm, D_GLA), F32),
            pltpu.VMEM((POOL_HALO + tm, D_POOL), F32),
            pltpu.VMEM((tm, d), BF16),
            pltpu.VMEM((GLA_HEADS, GLA_DV, GLA_DK), F32),
            pltpu.VMEM((LANES, d), BF16),
        ],
        compiler_params=pltpu.CompilerParams(
            dimension_semantics=("arbitrary", "arbitrary"),
            vmem_limit_bytes=VMEM_LIMIT_BYTES),
        name="mixer",
    )(x_head, x_tail, gain, win, wa, wa2, ba, poolw, pscale, hnorm, wout)


def _kv_kernel(mem_ref, gain_ref, wkv_ref, o_ref):
    h = _rms(mem_ref[...], gain_ref[...]).astype(BF16)
    o_ref[...] = _dot(h, wkv_ref[...]).astype(BF16)


def _mem_kv(mem, gain, wkv):
    t, d = mem.shape
    n = wkv.shape[1]
    return pl.pallas_call(
        _kv_kernel,
        grid=(n // KV_TN,),
        in_specs=[
            pl.BlockSpec((t, d), lambda j: (0, 0)),
            pl.BlockSpec((1, d), lambda j: (0, 0)),
            pl.BlockSpec((d, KV_TN), lambda j: (0, j)),
        ],
        out_specs=pl.BlockSpec((t, KV_TN), lambda j: (0, j)),
        out_shape=jax.ShapeDtypeStruct((t, n), BF16),
        compiler_params=pltpu.CompilerParams(
            dimension_semantics=("arbitrary",),
            vmem_limit_bytes=VMEM_LIMIT_BYTES),
        name="mem_kv",
    )(mem, gain, wkv)


def _xattn_kernel(x_ref, gain_ref, wq_ref, kv_ref, wo_ref, o_ref, cat_ref):
    half = x_ref.shape[0] // 2
    rows = [slice(0, half), slice(half, 2 * half)]
    head_cols = [slice(hd * XATTN_HEAD_DIM, (hd + 1) * XATTN_HEAD_DIM) for hd in range(XATTN_HEADS)]

    def query(r):
        h = _rms(x_ref[r, :], gain_ref[...]).astype(BF16)
        return (_dot(h, wq_ref[...]) * (XATTN_HEAD_DIM ** -0.5)).astype(BF16)

    def scores(q):
        return [lax.dot_general(q[:, c], kv_ref[:, c], (((1,), (1,)), ((), ())),
                                preferred_element_type=F32) for c in head_cols]

    def attend(r, logits):
        for c, lg in zip(head_cols, logits):
            e = jnp.exp(lg - jnp.max(lg, axis=-1, keepdims=True))
            p = e / jnp.sum(e, axis=-1, keepdims=True)
            v_h = kv_ref[:, D_MODEL + c.start:D_MODEL + c.stop]
            cat_ref[r, c] = _dot(p.astype(BF16), v_h).astype(BF16)

    def project_out(r):
        o_ref[r, :] = x_ref[r, :] + _dot(cat_ref[r, :], wo_ref[...])

    logits0 = scores(query(rows[0]))
    q1 = query(rows[1])
    attend(rows[0], logits0)
    logits1 = scores(q1)
    project_out(rows[0])
    attend(rows[1], logits1)
    project_out(rows[1])


def _xattn(x, gain, wq, kv, wo):
    b, s, d = x.shape
    m = kv.shape[1]
    tm = XATTN_TM
    return pl.pallas_call(
        _xattn_kernel,
        grid=(b, s // tm),
        in_specs=[
            pl.BlockSpec((None, tm, d), lambda i, j: (i, j, 0)),
            _resident((1, d)),
            _resident((d, d)),
            pl.BlockSpec((None, m, 2 * d), lambda i, j: (i, 0, 0)),
            _resident((d, d)),
        ],
        out_specs=pl.BlockSpec((None, tm, d), lambda i, j: (i, j, 0)),
        out_shape=jax.ShapeDtypeStruct((b, s, d), F32),
        scratch_shapes=[pltpu.VMEM((tm, d), BF16)],
        compiler_params=pltpu.CompilerParams(
            dimension_semantics=("arbitrary", "arbitrary"),
            vmem_limit_bytes=VMEM_LIMIT_BYTES),
        name="xattn",
    )(x, gain, wq, kv, wo)


def _layer(x, mem, ffn1_norm, ffn1_w_gate, ffn1_w_up, ffn1_w_down, mix_norm, w_in, pool_w,
           pool_scale, gla_w_a2, gla_b_a, gla_head_norm, w_out, xattn_norm, mem_norm,
           xattn_w_q, xattn_w_kv, xattn_w_o, ffn2_norm, ffn2_w_gate, ffn2_w_up, ffn2_w_down,
           final_gain, *, final_norm):
    b, s, d = x.shape
    row = lambda v: v.reshape(1, -1)
    bf = lambda w: w.astype(BF16)

    x = x.reshape(b * s, d)
    n_tiles = x.shape[0] // FFN_TM

    w_in_t = w_in.T
    x_head, w1, (w_a,) = _ffn(
        x, row(ffn1_norm), ffn1_w_gate, ffn1_w_up, ffn1_w_down, row(final_gain),
        final_norm=False, name="ffn_head", tiles=(0, 1), tf=FFN_HEAD_TF,
        side_casts=((w_in_t, MAIN_COLS, GLA_GATE_RANK, d),))
    later_weights = (ffn2_w_gate, ffn2_w_up, ffn2_w_down, w_out, xattn_w_q, xattn_w_kv, xattn_w_o)
    side_casts = tuple((w, 0, *w.shape) for w in later_weights) + ((w_in_t, 0, MAIN_COLS, d),)
    x_tail, _, (w2g, w2u, w2d, w_o_mix, w_q, w_kv, w_o_att, w_main) = _ffn(
        x, row(ffn1_norm), *w1, row(final_gain), final_norm=False, name="ffn",
        tiles=(1, n_tiles - 1), side_casts=side_casts)

    w_a2 = jnp.pad(gla_w_a2, ((0, LANES - GLA_GATE_RANK), (0, 0)))
    x = _mixer(x_head, x_tail, b, row(mix_norm), w_main, w_a, bf(w_a2), row(gla_b_a),
               bf(pool_w), row(pool_scale), row(gla_head_norm), w_o_mix)

    mb, ml, _ = mem.shape
    kv = _mem_kv(mem.reshape(mb * ml, d), row(mem_norm), w_kv).reshape(mb, ml, 2 * d)
    x = _xattn(x, row(xattn_norm), w_q, kv, w_o_att)

    x, _, _ = _ffn(x.reshape(b * s, d), row(ffn2_norm), w2g, w2u, w2d, row(final_gain),
                   final_norm=final_norm, name="ffn_final" if final_norm else "ffn_tail")
    return x.reshape(b, s, d)


def kernel(x, mem, ffn1_norm, ffn1_w_gate, ffn1_w_up, ffn1_w_down, mix_norm, w_in, pool_w, pool_scale, gla_w_a2, gla_b_a, gla_head_norm, w_out, xattn_norm, mem_norm, xattn_w_q, xattn_w_kv, xattn_w_o, ffn2_norm, ffn2_w_gate, ffn2_w_up, ffn2_w_down, final_norm):
    depth = ffn1_norm.shape[0]
    for l in range(depth):
        x = _layer(x, mem, ffn1_norm[l], ffn1_w_gate[l], ffn1_w_up[l], ffn1_w_down[l], mix_norm[l],
                   w_in[l], pool_w[l], pool_scale[l], gla_w_a2[l], gla_b_a[l], gla_head_norm[l],
                   w_out[l], xattn_norm[l], mem_norm[l], xattn_w_q[l], xattn_w_kv[l], xattn_w_o[l],
                   ffn2_norm[l], ffn2_w_gate[l], ffn2_w_up[l], ffn2_w_down[l], final_norm,
                   final_norm=(l == depth - 1))
    return x
```

```python
import functools

import jax
import jax.numpy as jnp
from jax import lax
from jax.experimental import pallas as pl
from jax.experimental.pallas import tpu as pltpu

F32 = jnp.float32
BF16 = jnp.bfloat16

D_MODEL = 2048
CHUNK = 64
D_POOL = 1024
POOL_WINDOWS = (2, 4, 8, 16)
POOL_GROUP = 256
POOL_HALO = 16
D_GLA = 1024
GLA_HEADS = 4
GLA_DV = 256
GLA_DK = 128
D_GLA_K = GLA_HEADS * GLA_DK
GLA_GATE_RANK = 16
GLA_GATE_TEMP = 16.0
XATTN_HEADS = 4
XATTN_HEAD_DIM = D_MODEL // XATTN_HEADS
RMS_EPS = 1e-6
LANES = 128
BF16_SUBLANES = 16
MXU_COLS = 256

Q_OFF = D_POOL
K_OFF = Q_OFF + D_GLA_K
V_OFF = K_OFF + D_GLA_K
G_OFF = V_OFF + D_GLA
MAIN_COLS = G_OFF + D_GLA

VMEM_LIMIT_BYTES = 62 * 1024 * 1024

FFN_TM = 1024
FFN_TF = 512
FFN_ROW_CHUNK = 256
FFN_HEAD_TF = 256
MIX_TM = 256
XATTN_TM = 512
KV_TN = 1024


def _rms(x, gain):
    return x * lax.rsqrt(jnp.mean(x * x, axis=-1, keepdims=True) + RMS_EPS) * gain


def _dot(a, b):
    return jnp.dot(a, b, preferred_element_type=F32)


def _resident(shape):
    return pl.BlockSpec(shape, lambda *_: (0,) * len(shape), pipeline_mode=pl.Buffered(1))


def _ffn_kernel(x_ref, gain_ref, wg_ref, wu_ref, wd_ref, fgain_ref, *rest, final_norm, n_side,
                emit_weights):
    n_w = 3 if emit_weights else 0
    side_in, o_ref, h_ref = rest[:n_side], rest[n_side], rest[-1]
    w_out = rest[n_side + 1:n_side + 1 + n_w]
    side_out = rest[n_side + 1 + n_w:2 * n_side + 1 + n_w]

    def weights():
        blocks = [ref[...] for ref in (wg_ref, wu_ref, wd_ref)]
        if emit_weights:
            blocks = [w.astype(BF16) for w in blocks]
            for dst_ref, w in zip(w_out, blocks):
                dst_ref[...] = w
        return blocks

    j = pl.program_id(1)
    last = pl.num_programs(1) - 1

    def activation(h, wg, wu, with_side_casts):
        g = _dot(h, wg)
        if with_side_casts:
            _side_cast(side_in, side_out)
        u = _dot(h, wu)
        return (g * jax.nn.sigmoid(g) * u).astype(BF16)

    def edge_step(first):
        wg, wu, wd = weights()
        chunks = [slice(r, r + FFN_ROW_CHUNK) for r in range(0, x_ref.shape[0], FFN_ROW_CHUNK)]

        def finish(rows, a):
            part = _dot(a, wd)
            if first:
                o_ref[rows, :] = part
            else:
                y = x_ref[rows, :] + 0.5 * (o_ref[rows, :] + part)
                if final_norm:
                    y = _rms(y, fgain_ref[...])
                o_ref[rows, :] = y

        pending = None
        for rows in chunks:
            if first:
                h = _rms(x_ref[rows, :], gain_ref[...]).astype(BF16)
                h_ref[rows, :] = h
            else:
                h = h_ref[rows, :]
            a = activation(h, wg, wu, with_side_casts=pending is None)
            if pending is not None:
                finish(*pending)
            pending = (rows, a)
        finish(*pending)

    @pl.when(j == 0)
    def _():
        edge_step(first=True)

    @pl.when(jnp.logical_and(j > 0, j < last))
    def _():
        wg, wu, wd = weights()
        h = h_ref[...]
        group = wg.shape[1] if side_in else MXU_COLS
        groups = [slice(c, c + group) for c in range(0, wg.shape[1], group)]
        acts = []
        for cols in groups:
            g = _dot(h, wg[:, cols])
            if cols.start == 0:
                _side_cast(side_in, side_out)
            u = _dot(h, wu[:, cols])
            acts.append((g * jax.nn.sigmoid(g) * u).astype(BF16))
        for cols, a in zip(groups, acts):
            o_ref[...] += _dot(a, wd[cols, :])

    @pl.when(j == last)
    def _():
        edge_step(first=False)


def _side_cast_block_rows(n_rows, n_steps):
    rows = BF16_SUBLANES
    while n_rows % rows or n_rows // rows > n_steps:
        rows += BF16_SUBLANES
    return rows


def _side_cast_specs(side_casts, grid):
    specs_in, specs_out, shapes = [], [], []
    for _, row0, n_rows, n_cols in side_casts:
        rows = _side_cast_block_rows(n_rows, grid[0] * grid[1])
        assert row0 % rows == 0

        def block(i, j, n_blocks=n_rows // rows):
            return jnp.minimum(i * grid[1] + j, n_blocks - 1)

        specs_in.append(pl.BlockSpec(
            (rows, n_cols), lambda i, j, block=block, first_block=row0 // rows: (first_block + block(i, j), 0)))
        specs_out.append(pl.BlockSpec((rows, n_cols), lambda i, j, block=block: (block(i, j), 0)))
        shapes.append(jax.ShapeDtypeStruct((n_rows, n_cols), BF16))
    return specs_in, specs_out, shapes


def _side_cast(src_refs, dst_refs):
    for src_ref, dst_ref in zip(src_refs, dst_refs):
        dst_ref[...] = src_ref[...].astype(BF16)


def _ffn(x, gain, wg, wu, wd, fgain, *, final_norm, name, tiles=None, tf=FFN_TF, side_casts=()):
    d = x.shape[1]
    dff = wg.shape[1]
    first, count = tiles if tiles is not None else (0, x.shape[0] // FFN_TM)
    t = count * FFN_TM
    emit_weights = wg.dtype == F32
    grid = (count, dff // tf)
    assert grid[1] >= 2
    w_specs = [
        pl.BlockSpec((d, tf), lambda i, j: (0, j)),
        pl.BlockSpec((d, tf), lambda i, j: (0, j)),
        pl.BlockSpec((tf, d), lambda i, j: (j, 0)),
    ]
    w_out_specs = w_specs if emit_weights else []
    w_out_shapes = [jax.ShapeDtypeStruct(w.shape, BF16) for w in (wg, wu, wd)] if emit_weights else []
    side_specs_in, side_specs_out, side_shapes = _side_cast_specs(side_casts, grid)
    outs = pl.pallas_call(
        functools.partial(_ffn_kernel, final_norm=final_norm, n_side=len(side_casts),
                          emit_weights=emit_weights),
        grid=grid,
        in_specs=[
            pl.BlockSpec((FFN_TM, d), lambda i, j: (first + i, 0)),
            pl.BlockSpec((1, d), lambda i, j: (0, 0)),
            *w_specs,
            pl.BlockSpec((1, d), lambda i, j: (0, 0)),
        ] + side_specs_in,
        out_specs=[pl.BlockSpec((FFN_TM, d), lambda i, j: (i, 0))] + w_out_specs + side_specs_out,
        out_shape=[jax.ShapeDtypeStruct((t, d), F32)] + w_out_shapes + side_shapes,
        scratch_shapes=[pltpu.VMEM((FFN_TM, d), BF16)],
        compiler_params=pltpu.CompilerParams(
            dimension_semantics=("arbitrary", "arbitrary"),
            vmem_limit_bytes=VMEM_LIMIT_BYTES),
        name=name,
    )(x, gain, wg, wu, wd, fgain, *[w for w, _, _, _ in side_casts])
    n_w = len(w_out_shapes)
    return outs[0], tuple(outs[1:1 + n_w]), tuple(outs[1 + n_w:])


def _mix_kernel(xh_ref, xt_ref, gain_ref, win_ref, wa_ref, wa2_ref, ba_ref, poolw_ref, pscale_ref,
                hnorm_ref, wout_ref, o_ref,
                proj_ref, oacc_ref, ubuf_ref, cat_ref, state_ref, wa_pad_ref, *, n_head_tiles):
    tm = o_ref.shape[0]
    s = pl.program_id(1)
    tile = pl.program_id(0) * pl.num_programs(1) + s
    in_head = tile < n_head_tiles

    def x_tile():
        return jnp.where(in_head, xh_ref[...], xt_ref[...])

    @pl.when(tile == 0)
    def _():
        wa_pad_ref[...] = jnp.zeros_like(wa_pad_ref)
        wa_pad_ref[0:GLA_GATE_RANK, :] = wa_ref[...]

    @pl.when(s == 0)
    def _():
        state_ref[...] = jnp.zeros_like(state_ref)
        ubuf_ref[0:POOL_HALO, :] = jnp.zeros((POOL_HALO, D_POOL), F32)

    h = _rms(x_tile(), gain_ref[...]).astype(BF16)

    def project(lo, width):
        proj_ref[:, lo:lo + width] = lax.dot_general(
            h, win_ref[lo:lo + width, :], (((1,), (1,)), ((), ())), preferred_element_type=F32)

    a_lr = lax.dot_general(h, wa_pad_ref[...], (((1,), (1,)), ((), ())),
                           preferred_element_type=F32)
    project(0, D_POOL)
    z = _dot(a_lr.astype(BF16), wa2_ref[...]) + ba_ref[...]
    project(K_OFF, D_GLA_K)
    la = (jnp.minimum(z, 0.0) - jnp.log1p(jnp.exp(-jnp.abs(z)))) * (1.0 / GLA_GATE_TEMP)
    row = lax.broadcasted_iota(jnp.int32, (tm, tm), 0)
    col = lax.broadcasted_iota(jnp.int32, (tm, tm), 1)
    same_chunk = (row // CHUNK) == (col // CHUNK)
    chunk_sum = same_chunk.astype(BF16)
    chunk_prefix = (same_chunk & (row >= col)).astype(BF16)
    la_hi = la.astype(BF16)
    la_lo = (la - la_hi.astype(F32)).astype(BF16)
    sums = _dot(jnp.concatenate([chunk_prefix, chunk_sum], axis=0),
                jnp.concatenate([la_hi, la_lo], axis=1))
    cum = sums[:tm, :D_GLA_K] + sums[:tm, D_GLA_K:]
    b_end = sums[tm:, :D_GLA_K] + sums[tm:, D_GLA_K:]
    project(V_OFF, D_GLA)

    ubuf_ref[POOL_HALO:POOL_HALO + tm, :] = proj_ref[:, 0:D_POOL]
    t_idx = s * tm + lax.broadcasted_iota(jnp.int32, (tm, 1), 0)
    count = (t_idx + 1).astype(F32)
    for gi, w in enumerate(POOL_WINDOWS):
        cols = slice(gi * POOL_GROUP, (gi + 1) * POOL_GROUP)
        ext = ubuf_ref[:, cols]
        acc = ext
        shift = 1
        while shift < w:
            acc = acc + pltpu.roll(acc, shift, 0)
            shift *= 2
        win = acc[POOL_HALO:, :]
        d = win / jnp.minimum(count, float(w)) - ext[POOL_HALO:, :]
        y = _dot(d.astype(BF16), poolw_ref[gi]) * pscale_ref[:, cols]
        cat_ref[:, cols] = y.astype(BF16)
    ubuf_ref[0:POOL_HALO, :] = ubuf_ref[tm:tm + POOL_HALO, :]

    k_dec = (proj_ref[:, K_OFF:K_OFF + D_GLA_K] * jnp.exp(b_end - cum)).astype(BF16)
    decay = jnp.exp(b_end)
    n_chunks = tm // CHUNK
    row_chunk = lax.broadcasted_iota(jnp.int32, (tm, GLA_DK), 0) // CHUNK

    def chunk_blocks(a):
        return jnp.concatenate(
            [jnp.where(row_chunk == c, a, jnp.zeros_like(a)) for c in range(n_chunks)], axis=1)

    upd = {}
    for hd in range(GLA_HEADS):
        v_h = proj_ref[:, V_OFF + hd * GLA_DV:V_OFF + (hd + 1) * GLA_DV].astype(BF16)
        upd[hd] = lax.dot_general(v_h, chunk_blocks(k_dec[:, hd * GLA_DK:(hd + 1) * GLA_DK]),
                                  (((0,), (0,)), ((), ())),
                                  preferred_element_type=F32)
    project(Q_OFF, D_GLA_K)
    project(G_OFF, D_GLA)
    states = {}
    for hd in range(GLA_HEADS):
        st = state_ref[hd]
        for c in range(n_chunks):
            st = (st * decay[c * CHUNK:c * CHUNK + 1, hd * GLA_DK:(hd + 1) * GLA_DK]
                  + upd[hd][:, c * GLA_DK:(c + 1) * GLA_DK])
            states[hd, c] = st.astype(BF16)
        state_ref[hd] = st
    q = (proj_ref[:, Q_OFF:Q_OFF + D_GLA_K] * (GLA_DK ** -0.5)).astype(BF16)
    for hd in range(GLA_HEADS):
        s_all = jnp.concatenate([states[hd, c] for c in range(n_chunks)], axis=1)
        oacc_ref[:, hd * GLA_DV:(hd + 1) * GLA_DV] = lax.dot_general(
            chunk_blocks(q[:, hd * GLA_DK:(hd + 1) * GLA_DK]), s_all, (((1,), (1,)), ((), ())),
            preferred_element_type=F32)
    o_ref[...] = x_tile() + _dot(cat_ref[:, 0:D_POOL], wout_ref[0:D_POOL, :])
    for hd in range(GLA_HEADS):
        hc = slice(hd * GLA_DV, (hd + 1) * GLA_DV)
        o = oacc_ref[:, hc]
        o = o * lax.rsqrt(jnp.mean(o * o, axis=-1, keepdims=True) + RMS_EPS) * hnorm_ref[:, hc]
        g_h = proj_ref[:, G_OFF + hd * GLA_DV:G_OFF + (hd + 1) * GLA_DV]
        y = o * (g_h * jax.nn.sigmoid(g_h))
        cat_ref[:, D_POOL + hd * GLA_DV:D_POOL + (hd + 1) * GLA_DV] = y.astype(BF16)

    o_ref[...] += _dot(cat_ref[:, D_POOL:], wout_ref[D_POOL:, :])


def _mixer(x_head, x_tail, b, gain, win, wa, wa2, ba, poolw, pscale, hnorm, wout):
    d = x_head.shape[1]
    tm = MIX_TM
    n_head = x_head.shape[0] // tm
    s = (x_head.shape[0] + x_tail.shape[0]) // b
    n_s = s // tm
    return pl.pallas_call(
        functools.partial(_mix_kernel, n_head_tiles=n_head),
        grid=(b, n_s),
        in_specs=[
            pl.BlockSpec((tm, d), lambda i, j: (jnp.minimum(i * n_s + j, n_head - 1), 0)),
            pl.BlockSpec((tm, d), lambda i, j: (jnp.maximum(i * n_s + j - n_head, 0), 0)),
            _resident((1, d)),
            _resident((MAIN_COLS, d)),
            _resident((GLA_GATE_RANK, d)),
            _resident((LANES, D_GLA_K)),
            _resident((1, D_GLA_K)),
            _resident((len(POOL_WINDOWS), POOL_GROUP, POOL_GROUP)),
            _resident((1, D_POOL)),
            _resident((1, D_GLA)),
            _resident((d, d)),
        ],
        out_specs=pl.BlockSpec((None, tm, d), lambda i, j: (i, j, 0)),
        out_shape=jax.ShapeDtypeStruct((b, s, d), F32),
        scratch_shapes=[
            pltpu.VMEM((tm, MAIN_COLS), F32),
            pltpu.VMEM((tm, D_GLA), F32),
            pltpu.VMEM((POOL_HALO + tm, D_POOL), F32),
            pltpu.VMEM((tm, d), BF16),
            pltpu.VMEM((GLA_HEADS, GLA_DV, GLA_DK), F32),
            pltpu.VMEM((LANES, d), BF16),
        ],
        compiler_params=pltpu.CompilerParams(
            dimension_semantics=("arbitrary", "arbitrary"),
            vmem_limit_bytes=VMEM_LIMIT_BYTES),
        name="mixer",
    )(x_head, x_tail, gain, win, wa, wa2, ba, poolw, pscale, hnorm, wout)


def _kv_kernel(mem_ref, gain_ref, wkv_ref, o_ref):
    h = _rms(mem_ref[...], gain_ref[...]).astype(BF16)
    o_ref[...] = _dot(h, wkv_ref[...]).astype(BF16)


def _mem_kv(mem, gain, wkv):
    t, d = mem.shape
    n = wkv.shape[1]
    return pl.pallas_call(
        _kv_kernel,
        grid=(n // KV_TN,),
        in_specs=[
            pl.BlockSpec((t, d), lambda j: (0, 0)),
            pl.BlockSpec((1, d), lambda j: (0, 0)),
            pl.BlockSpec((d, KV_TN), lambda j: (0, j)),
        ],
        out_specs=pl.BlockSpec((t, KV_TN), lambda j: (0, j)),
        out_shape=jax.ShapeDtypeStruct((t, n), BF16),
        compiler_params=pltpu.CompilerParams(
            dimension_semantics=("arbitrary",),
            vmem_limit_bytes=VMEM_LIMIT_BYTES),
        name="mem_kv",
    )(mem, gain, wkv)


def _xattn_kernel(x_ref, gain_ref, wq_ref, kv_ref, wo_ref, o_ref, cat_ref):
    half = x_ref.shape[0] // 2
    rows = [slice(0, half), slice(half, 2 * half)]
    head_cols = [slice(hd * XATTN_HEAD_DIM, (hd + 1) * XATTN_HEAD_DIM) for hd in range(XATTN_HEADS)]

    def query(r):
        h = _rms(x_ref[r, :], gain_ref[...]).astype(BF16)
        return (_dot(h, wq_ref[...]) * (XATTN_HEAD_DIM ** -0.5)).astype(BF16)

    def scores(q):
        return [lax.dot_general(q[:, c], kv_ref[:, c], (((1,), (1,)), ((), ())),
                                preferred_element_type=F32) for c in head_cols]

    def attend(r, logits):
        for c, lg in zip(head_cols, logits):
            e = jnp.exp(lg - jnp.max(lg, axis=-1, keepdims=True))
            p = e / jnp.sum(e, axis=-1, keepdims=True)
            v_h = kv_ref[:, D_MODEL + c.start:D_MODEL + c.stop]
            cat_ref[r, c] = _dot(p.astype(BF16), v_h).astype(BF16)

    def project_out(r):
        o_ref[r, :] = x_ref[r, :] + _dot(cat_ref[r, :], wo_ref[...])

    logits0 = scores(query(rows[0]))
    q1 = query(rows[1])
    attend(rows[0], logits0)
    logits1 = scores(q1)
    project_out(rows[0])
    attend(rows[1], logits1)
    project_out(rows[1])


def _xattn(x, gain, wq, kv, wo):
    b, s, d = x.shape
    m = kv.shape[1]
    tm = XATTN_TM
    return pl.pallas_call(
        _xattn_kernel,
        grid=(b, s // tm),
        in_specs=[
            pl.BlockSpec((None, tm, d), lambda i, j: (i, j, 0)),
            _resident((1, d)),
            _resident((d, d)),
            pl.BlockSpec((None, m, 2 * d), lambda i, j: (i, 0, 0)),
            _resident((d, d)),
        ],
        out_specs=pl.BlockSpec((None, tm, d), lambda i, j: (i, j, 0)),
        out_shape=jax.ShapeDtypeStruct((b, s, d), F32),
        scratch_shapes=[pltpu.VMEM((tm, d), BF16)],
        compiler_params=pltpu.CompilerParams(
            dimension_semantics=("arbitrary", "arbitrary"),
            vmem_limit_bytes=VMEM_LIMIT_BYTES),
        name="xattn",
    )(x, gain, wq, kv, wo)


def _layer(x, mem, ffn1_norm, ffn1_w_gate, ffn1_w_up, ffn1_w_down, mix_norm, w_in, pool_w,
           pool_scale, gla_w_a2, gla_b_a, gla_head_norm, w_out, xattn_norm, mem_norm,
           xattn_w_q, xattn_w_kv, xattn_w_o, ffn2_norm, ffn2_w_gate, ffn2_w_up, ffn2_w_down,
           final_gain, *, final_norm):
    b, s, d = x.shape
    row = lambda v: v.reshape(1, -1)
    bf = lambda w: w.astype(BF16)

    x = x.reshape(b * s, d)
    n_tiles = x.shape[0] // FFN_TM

    w_in_t = w_in.T
    x_head, w1, (w_a,) = _ffn(
        x, row(ffn1_norm), ffn1_w_gate, ffn1_w_up, ffn1_w_down, row(final_gain),
        final_norm=False, name="ffn_head", tiles=(0, 1), tf=FFN_HEAD_TF,
        side_casts=((w_in_t, MAIN_COLS, GLA_GATE_RANK, d),))
    later_weights = (ffn2_w_gate, ffn2_w_up, ffn2_w_down, w_out, xattn_w_q, xattn_w_kv, xattn_w_o)
    side_casts = tuple((w, 0, *w.shape) for w in later_weights) + ((w_in_t, 0, MAIN_COLS, d),)
    x_tail, _, (w2g, w2u, w2d, w_o_mix, w_q, w_kv, w_o_att, w_main) = _ffn(
        x, row(ffn1_norm), *w1, row(final_gain), final_norm=False, name="ffn",
        tiles=(1, n_tiles - 1), side_casts=side_casts)

    w_a2 = jnp.pad(gla_w_a2, ((0, LANES - GLA_GATE_RANK), (0, 0)))
    x = _mixer(x_head, x_tail, b, row(mix_norm), w_main, w_a, bf(w_a2), row(gla_b_a),
               bf(pool_w), row(pool_scale), row(gla_head_norm), w_o_mix)

    mb, ml, _ = mem.shape
    kv = _mem_kv(mem.reshape(mb * ml, d), row(mem_norm), w_kv).reshape(mb, ml, 2 * d)
    x = _xattn(x, row(xattn_norm), w_q, kv, w_o_att)

    x, _, _ = _ffn(x.reshape(b * s, d), row(ffn2_norm), w2g, w2u, w2d, row(final_gain),
                   final_norm=final_norm, name="ffn_final" if final_norm else "ffn_tail")
    return x.reshape(b, s, d)


def kernel(x, mem, ffn1_norm, ffn1_w_gate, ffn1_w_up, ffn1_w_down, mix_norm, w_in, pool_w, pool_scale, gla_w_a2, gla_b_a, gla_head_norm, w_out, xattn_norm, mem_norm, xattn_w_q, xattn_w_kv, xattn_w_o, ffn2_norm, ffn2_w_gate, ffn2_w_up, ffn2_w_down, final_norm):
    depth = ffn1_norm.shape[0]
    for l in range(depth):
        x = _layer(x, mem, ffn1_norm[l], ffn1_w_gate[l], ffn1_w_up[l], ffn1_w_down[l], mix_norm[l],
                   w_in[l], pool_w[l], pool_scale[l], gla_w_a2[l], gla_b_a[l], gla_head_norm[l],
                   w_out[l], xattn_norm[l], mem_norm[l], xattn_w_q[l], xattn_w_kv[l], xattn_w_o[l],
                   ffn2_norm[l], ffn2_w_gate[l], ffn2_w_up[l], ffn2_w_down[l], final_norm,
                   final_norm=(l == depth - 1))
    return x
```

```python
import functools

import jax
import jax.numpy as jnp
from jax import lax
from jax.experimental import pallas as pl
from jax.experimental.pallas import tpu as pltpu

F32 = jnp.float32
BF16 = jnp.bfloat16

D_MODEL = 2048
CHUNK = 64
D_POOL = 1024
POOL_WINDOWS = (2, 4, 8, 16)
POOL_GROUP = 256
POOL_HALO = 16
D_GLA = 1024
GLA_HEADS = 4
GLA_DV = 256
GLA_DK = 128
D_GLA_K = GLA_HEADS * GLA_DK
GLA_GATE_RANK = 16
GLA_GATE_TEMP = 16.0
XATTN_HEADS = 4
XATTN_HEAD_DIM = D_MODEL // XATTN_HEADS
RMS_EPS = 1e-6
LANES = 128
BF16_SUBLANES = 16
MXU_COLS = 256

Q_OFF = D_POOL
K_OFF = Q_OFF + D_GLA_K
V_OFF = K_OFF + D_GLA_K
G_OFF = V_OFF + D_GLA
MAIN_COLS = G_OFF + D_GLA

VMEM_LIMIT_BYTES = 62 * 1024 * 1024

FFN_TM = 1024
FFN_TF = 512
FFN_ROW_CHUNK = 256
FFN_HEAD_TF = 256
MIX_TM = 256
XATTN_TM = 512
KV_TN = 1024


def _rms(x, gain):
    return x * lax.rsqrt(jnp.mean(x * x, axis=-1, keepdims=True) + RMS_EPS) * gain


def _dot(a, b):
    return jnp.dot(a, b, preferred_element_type=F32)


def _resident(shape):
    return pl.BlockSpec(shape, lambda *_: (0,) * len(shape), pipeline_mode=pl.Buffered(1))


def _ffn_kernel(x_ref, gain_ref, wg_ref, wu_ref, wd_ref, fgain_ref, *rest, final_norm, n_side,
                emit_weights):
    n_w = 3 if emit_weights else 0
    side_in, o_ref, h_ref = rest[:n_side], rest[n_side], rest[-1]
    w_out = rest[n_side + 1:n_side + 1 + n_w]
    side_out = rest[n_side + 1 + n_w:2 * n_side + 1 + n_w]

    def weights():
        blocks = [ref[...] for ref in (wg_ref, wu_ref, wd_ref)]
        if emit_weights:
            blocks = [w.astype(BF16) for w in blocks]
            for dst_ref, w in zip(w_out, blocks):
                dst_ref[...] = w
        return blocks

    j = pl.program_id(1)
    last = pl.num_programs(1) - 1

    def activation(h, wg, wu, with_side_casts):
        g = _dot(h, wg)
        if with_side_casts:
            _side_cast(side_in, side_out)
        u = _dot(h, wu)
        return (g * jax.nn.sigmoid(g) * u).astype(BF16)

    def edge_step(first):
        wg, wu, wd = weights()
        chunks = [slice(r, r + FFN_ROW_CHUNK) for r in range(0, x_ref.shape[0], FFN_ROW_CHUNK)]

        def finish(rows, a):
            part = _dot(a, wd)
            if first:
                o_ref[rows, :] = part
            else:
                y = x_ref[rows, :] + 0.5 * (o_ref[rows, :] + part)
                if final_norm:
                    y = _rms(y, fgain_ref[...])
                o_ref[rows, :] = y

        pending = None
        for rows in chunks:
            if first:
                h = _rms(x_ref[rows, :], gain_ref[...]).astype(BF16)
                h_ref[rows, :] = h
            else:
                h = h_ref[rows, :]
            a = activation(h, wg, wu, with_side_casts=pending is None)
            if pending is not None:
                finish(*pending)
            pending = (rows, a)
        finish(*pending)

    @pl.when(j == 0)
    def _():
        edge_step(first=True)

    @pl.when(jnp.logical_and(j > 0, j < last))
    def _():
        wg, wu, wd = weights()
        h = h_ref[...]
        groups = [slice(c, c + MXU_COLS) for c in range(0, wg.shape[1], MXU_COLS)]
        acts = []
        for cols in groups:
            g = _dot(h, wg[:, cols])
            if cols.start == 0:
                _side_cast(side_in, side_out)
            u = _dot(h, wu[:, cols])
            acts.append((g * jax.nn.sigmoid(g) * u).astype(BF16))
        for cols, a in zip(groups, acts):
            o_ref[...] += _dot(a, wd[cols, :])

    @pl.when(j == last)
    def _():
        edge_step(first=False)


def _side_cast_block_rows(n_rows, n_steps):
    rows = BF16_SUBLANES
    while n_rows % rows or n_rows // rows > n_steps:
        rows += BF16_SUBLANES
    return rows


def _side_cast_specs(side_casts, grid):
    specs_in, specs_out, shapes = [], [], []
    for _, row0, n_rows, n_cols in side_casts:
        rows = _side_cast_block_rows(n_rows, grid[0] * grid[1])
        assert row0 % rows == 0

        def block(i, j, n_blocks=n_rows // rows):
            return jnp.minimum(i * grid[1] + j, n_blocks - 1)

        specs_in.append(pl.BlockSpec(
            (rows, n_cols), lambda i, j, block=block, first_block=row0 // rows: (first_block + block(i, j), 0)))
        specs_out.append(pl.BlockSpec((rows, n_cols), lambda i, j, block=block: (block(i, j), 0)))
        shapes.append(jax.ShapeDtypeStruct((n_rows, n_cols), BF16))
    return specs_in, specs_out, shapes


def _side_cast(src_refs, dst_refs):
    for src_ref, dst_ref in zip(src_refs, dst_refs):
        dst_ref[...] = src_ref[...].astype(BF16)


def _ffn(x, gain, wg, wu, wd, fgain, *, final_norm, name, tiles=None, tf=FFN_TF, side_casts=()):
    d = x.shape[1]
    dff = wg.shape[1]
    first, count = tiles if tiles is not None else (0, x.shape[0] // FFN_TM)
    t = count * FFN_TM
    emit_weights = wg.dtype == F32
    grid = (count, dff // tf)
    assert grid[1] >= 2
    w_specs = [
        pl.BlockSpec((d, tf), lambda i, j: (0, j)),
        pl.BlockSpec((d, tf), lambda i, j: (0, j)),
        pl.BlockSpec((tf, d), lambda i, j: (j, 0)),
    ]
    w_out_specs = w_specs if emit_weights else []
    w_out_shapes = [jax.ShapeDtypeStruct(w.shape, BF16) for w in (wg, wu, wd)] if emit_weights else []
    side_specs_in, side_specs_out, side_shapes = _side_cast_specs(side_casts, grid)
    outs = pl.pallas_call(
        functools.partial(_ffn_kernel, final_norm=final_norm, n_side=len(side_casts),
                          emit_weights=emit_weights),
        grid=grid,
        in_specs=[
            pl.BlockSpec((FFN_TM, d), lambda i, j: (first + i, 0)),
            pl.BlockSpec((1, d), lambda i, j: (0, 0)),
            *w_specs,
            pl.BlockSpec((1, d), lambda i, j: (0, 0)),
        ] + side_specs_in,
        out_specs=[pl.BlockSpec((FFN_TM, d), lambda i, j: (i, 0))] + w_out_specs + side_specs_out,
        out_shape=[jax.ShapeDtypeStruct((t, d), F32)] + w_out_shapes + side_shapes,
        scratch_shapes=[pltpu.VMEM((FFN_TM, d), BF16)],
        compiler_params=pltpu.CompilerParams(
            dimension_semantics=("arbitrary", "arbitrary"),
            vmem_limit_bytes=VMEM_LIMIT_BYTES),
        name=name,
    )(x, gain, wg, wu, wd, fgain, *[w for w, _, _, _ in side_casts])
    n_w = len(w_out_shapes)
    return outs[0], tuple(outs[1:1 + n_w]), tuple(outs[1 + n_w:])


def _mix_kernel(xh_ref, xt_ref, xh_next_ref, xt_next_ref, gain_ref, win_ref, wa_ref, wa2_ref, ba_ref,
                poolw_ref, pscale_ref, hnorm_ref, wout_ref, o_ref,
                proj_ref, oacc_ref, ubuf_ref, cat_ref, state_ref, wa_pad_ref, h_ref,
                *, n_head_tiles):
    tm = o_ref.shape[0]
    s = pl.program_id(1)
    tile = pl.program_id(0) * pl.num_programs(1) + s
    in_head = tile < n_head_tiles
    next_in_head = tile + 1 < n_head_tiles

    def x_tile():
        return jnp.where(in_head, xh_ref[...], xt_ref[...])

    def project(h, lo, width):
        proj_ref[:, lo:lo + width] = lax.dot_general(
            h, win_ref[lo:lo + width, :], (((1,), (1,)), ((), ())), preferred_element_type=F32)

    def start_tile(x):
        h_ref[...] = _rms(x, gain_ref[...]).astype(BF16)
        project(h_ref[...], 0, D_POOL)

    @pl.when(tile == 0)
    def _():
        wa_pad_ref[...] = jnp.zeros_like(wa_pad_ref)
        wa_pad_ref[0:GLA_GATE_RANK, :] = wa_ref[...]
        start_tile(x_tile())

    @pl.when(s == 0)
    def _():
        state_ref[...] = jnp.zeros_like(state_ref)
        ubuf_ref[0:POOL_HALO, :] = jnp.zeros((POOL_HALO, D_POOL), F32)

    h = h_ref[...]
    a_lr = lax.dot_general(h, wa_pad_ref[...], (((1,), (1,)), ((), ())),
                           preferred_element_type=F32)
    project(h, K_OFF, D_GLA_K)
    z = _dot(a_lr.astype(BF16), wa2_ref[...]) + ba_ref[...]
    project(h, V_OFF, D_GLA)
    la = (jnp.minimum(z, 0.0) - jnp.log1p(jnp.exp(-jnp.abs(z)))) * (1.0 / GLA_GATE_TEMP)
    row = lax.broadcasted_iota(jnp.int32, (tm, tm), 0)
    col = lax.broadcasted_iota(jnp.int32, (tm, tm), 1)
    same_chunk = (row // CHUNK) == (col // CHUNK)
    chunk_sum = same_chunk.astype(BF16)
    chunk_prefix = (same_chunk & (row >= col)).astype(BF16)
    la_hi = la.astype(BF16)
    la_lo = (la - la_hi.astype(F32)).astype(BF16)
    sums = _dot(jnp.concatenate([chunk_prefix, chunk_sum], axis=0),
                jnp.concatenate([la_hi, la_lo], axis=1))
    cum = sums[:tm, :D_GLA_K] + sums[:tm, D_GLA_K:]
    b_end = sums[tm:, :D_GLA_K] + sums[tm:, D_GLA_K:]

    ubuf_ref[POOL_HALO:POOL_HALO + tm, :] = proj_ref[:, 0:D_POOL]
    t_idx = s * tm + lax.broadcasted_iota(jnp.int32, (tm, 1), 0)
    count = (t_idx + 1).astype(F32)
    for gi, w in enumerate(POOL_WINDOWS):
        cols = slice(gi * POOL_GROUP, (gi + 1) * POOL_GROUP)
        ext = ubuf_ref[:, cols]
        acc = ext
        shift = 1
        while shift < w:
            acc = acc + pltpu.roll(acc, shift, 0)
            shift *= 2
        win = acc[POOL_HALO:, :]
        d = win / jnp.minimum(count, float(w)) - ext[POOL_HALO:, :]
        y = _dot(d.astype(BF16), poolw_ref[gi]) * pscale_ref[:, cols]
        cat_ref[:, cols] = y.astype(BF16)
    ubuf_ref[0:POOL_HALO, :] = ubuf_ref[tm:tm + POOL_HALO, :]

    k_dec = (proj_ref[:, K_OFF:K_OFF + D_GLA_K] * jnp.exp(b_end - cum)).astype(BF16)
    decay = jnp.exp(b_end)
    n_chunks = tm // CHUNK
    row_chunk = lax.broadcasted_iota(jnp.int32, (tm, GLA_DK), 0) // CHUNK

    def chunk_blocks(a):
        return jnp.concatenate(
            [jnp.where(row_chunk == c, a, jnp.zeros_like(a)) for c in range(n_chunks)], axis=1)

    upd = {}
    for hd in range(GLA_HEADS):
        v_h = proj_ref[:, V_OFF + hd * GLA_DV:V_OFF + (hd + 1) * GLA_DV].astype(BF16)
        upd[hd] = lax.dot_general(v_h, chunk_blocks(k_dec[:, hd * GLA_DK:(hd + 1) * GLA_DK]),
                                  (((0,), (0,)), ((), ())),
                                  preferred_element_type=F32)
    project(h, Q_OFF, D_GLA_K)
    project(h, G_OFF, D_GLA)
    states = {}
    for hd in range(GLA_HEADS):
        st = state_ref[hd]
        for c in range(n_chunks):
            st = (st * decay[c * CHUNK:c * CHUNK + 1, hd * GLA_DK:(hd + 1) * GLA_DK]
                  + upd[hd][:, c * GLA_DK:(c + 1) * GLA_DK])
            states[hd, c] = st.astype(BF16)
        state_ref[hd] = st
    q = (proj_ref[:, Q_OFF:Q_OFF + D_GLA_K] * (GLA_DK ** -0.5)).astype(BF16)
    for hd in range(GLA_HEADS):
        s_all = jnp.concatenate([states[hd, c] for c in range(n_chunks)], axis=1)
        oacc_ref[:, hd * GLA_DV:(hd + 1) * GLA_DV] = lax.dot_general(
            chunk_blocks(q[:, hd * GLA_DK:(hd + 1) * GLA_DK]), s_all, (((1,), (1,)), ((), ())),
            preferred_element_type=F32)
    o_ref[...] = x_tile() + _dot(cat_ref[:, 0:D_POOL], wout_ref[0:D_POOL, :])
    for hd in range(GLA_HEADS):
        hc = slice(hd * GLA_DV, (hd + 1) * GLA_DV)
        o = oacc_ref[:, hc]
        o = o * lax.rsqrt(jnp.mean(o * o, axis=-1, keepdims=True) + RMS_EPS) * hnorm_ref[:, hc]
        g_h = proj_ref[:, G_OFF + hd * GLA_DV:G_OFF + (hd + 1) * GLA_DV]
        y = o * (g_h * jax.nn.sigmoid(g_h))
        cat_ref[:, D_POOL + hd * GLA_DV:D_POOL + (hd + 1) * GLA_DV] = y.astype(BF16)

    o_ref[...] += _dot(cat_ref[:, D_POOL:], wout_ref[D_POOL:, :])

    start_tile(jnp.where(next_in_head, xh_next_ref[...], xt_next_ref[...]))


def _mixer(x_head, x_tail, b, gain, win, wa, wa2, ba, poolw, pscale, hnorm, wout):
    d = x_head.shape[1]
    tm = MIX_TM
    n_head = x_head.shape[0] // tm
    s = (x_head.shape[0] + x_tail.shape[0]) // b
    n_s = s // tm
    n_tiles = b * n_s

    def x_specs(ahead):
        def tile(i, j):
            return jnp.minimum(i * n_s + j + ahead, n_tiles - 1)
        return [pl.BlockSpec((tm, d), lambda i, j: (jnp.minimum(tile(i, j), n_head - 1), 0)),
                pl.BlockSpec((tm, d), lambda i, j: (jnp.maximum(tile(i, j) - n_head, 0), 0))]

    return pl.pallas_call(
        functools.partial(_mix_kernel, n_head_tiles=n_head),
        grid=(b, n_s),
        in_specs=[
            *x_specs(0),
            *x_specs(1),
            _resident((1, d)),
            _resident((MAIN_COLS, d)),
            _resident((GLA_GATE_RANK, d)),
            _resident((LANES, D_GLA_K)),
            _resident((1, D_GLA_K)),
            _resident((len(POOL_WINDOWS), POOL_GROUP, POOL_GROUP)),
            _resident((1, D_POOL)),
            _resident((1, D_GLA)),
            _resident((d, d)),
        ],
        out_specs=pl.BlockSpec((None, tm, d), lambda i, j: (i, j, 0)),
        out_shape=jax.ShapeDtypeStruct((b, s, d), F32),
        scratch_shapes=[
            pltpu.VMEM((tm, MAIN_COLS), F32),
            pltpu.VMEM((tm, D_GLA), F32),
            pltpu.VMEM((POOL_HALO + tm, D_POOL), F32),
            pltpu.VMEM((tm, d), BF16),
            pltpu.VMEM((GLA_HEADS, GLA_DV, GLA_DK), F32),
            pltpu.VMEM((LANES, d), BF16),
            pltpu.VMEM((tm, d), BF16),
        ],
        compiler_params=pltpu.CompilerParams(
            dimension_semantics=("arbitrary", "arbitrary"),
            vmem_limit_bytes=VMEM_LIMIT_BYTES),
        name="mixer",
    )(x_head, x_tail, x_head, x_tail, gain, win, wa, wa2, ba, poolw, pscale, hnorm, wout)


def _kv_kernel(mem_ref, gain_ref, wkv_ref, o_ref):
    h = _rms(mem_ref[...], gain_ref[...]).astype(BF16)
    o_ref[...] = _dot(h, wkv_ref[...]).astype(BF16)


def _mem_kv(mem, gain, wkv):
    t, d = mem.shape
    n = wkv.shape[1]
    return pl.pallas_call(
        _kv_kernel,
        grid=(n // KV_TN,),
        in_specs=[
            pl.BlockSpec((t, d), lambda j: (0, 0)),
            pl.BlockSpec((1, d), lambda j: (0, 0)),
            pl.BlockSpec((d, KV_TN), lambda j: (0, j)),
        ],
        out_specs=pl.BlockSpec((t, KV_TN), lambda j: (0, j)),
        out_shape=jax.ShapeDtypeStruct((t, n), BF16),
        compiler_params=pltpu.CompilerParams(
            dimension_semantics=("arbitrary",),
            vmem_limit_bytes=VMEM_LIMIT_BYTES),
        name="mem_kv",
    )(mem, gain, wkv)


def _xattn_kernel(x_ref, gain_ref, wq_ref, kv_ref, wo_ref, o_ref, cat_ref):
    half = x_ref.shape[0] // 2
    rows = [slice(0, half), slice(half, 2 * half)]
    head_cols = [slice(hd * XATTN_HEAD_DIM, (hd + 1) * XATTN_HEAD_DIM) for hd in range(XATTN_HEADS)]

    def query(r):
        h = _rms(x_ref[r, :], gain_ref[...]).astype(BF16)
        return (_dot(h, wq_ref[...]) * (XATTN_HEAD_DIM ** -0.5)).astype(BF16)

    def scores(q):
        return [lax.dot_general(q[:, c], kv_ref[:, c], (((1,), (1,)), ((), ())),
                                preferred_element_type=F32) for c in head_cols]

    def attend(r, logits):
        for c, lg in zip(head_cols, logits):
            e = jnp.exp(lg - jnp.max(lg, axis=-1, keepdims=True))
            p = e / jnp.sum(e, axis=-1, keepdims=True)
            v_h = kv_ref[:, D_MODEL + c.start:D_MODEL + c.stop]
            cat_ref[r, c] = _dot(p.astype(BF16), v_h).astype(BF16)

    def project_out(r):
        o_ref[r, :] = x_ref[r, :] + _dot(cat_ref[r, :], wo_ref[...])

    logits0 = scores(query(rows[0]))
    q1 = query(rows[1])
    attend(rows[0], logits0)
    logits1 = scores(q1)
    project_out(rows[0])
    attend(rows[1], logits1)
    project_out(rows[1])


def _xattn(x, gain, wq, kv, wo):
    b, s, d = x.shape
    m = kv.shape[1]
    tm = XATTN_TM
    return pl.pallas_call(
        _xattn_kernel,
        grid=(b, s // tm),
        in_specs=[
            pl.BlockSpec((None, tm, d), lambda i, j: (i, j, 0)),
            _resident((1, d)),
            _resident((d, d)),
            pl.BlockSpec((None, m, 2 * d), lambda i, j: (i, 0, 0)),
            _resident((d, d)),
        ],
        out_specs=pl.BlockSpec((None, tm, d), lambda i, j: (i, j, 0)),
        out_shape=jax.ShapeDtypeStruct((b, s, d), F32),
        scratch_shapes=[pltpu.VMEM((tm, d), BF16)],
        compiler_params=pltpu.CompilerParams(
            dimension_semantics=("arbitrary", "arbitrary"),
            vmem_limit_bytes=VMEM_LIMIT_BYTES),
        name="xattn",
    )(x, gain, wq, kv, wo)


def _layer(x, mem, ffn1_norm, ffn1_w_gate, ffn1_w_up, ffn1_w_down, mix_norm, w_in, pool_w,
           pool_scale, gla_w_a2, gla_b_a, gla_head_norm, w_out, xattn_norm, mem_norm,
           xattn_w_q, xattn_w_kv, xattn_w_o, ffn2_norm, ffn2_w_gate, ffn2_w_up, ffn2_w_down,
           final_gain, *, final_norm):
    b, s, d = x.shape
    row = lambda v: v.reshape(1, -1)
    bf = lambda w: w.astype(BF16)

    x = x.reshape(b * s, d)
    n_tiles = x.shape[0] // FFN_TM

    w_in_t = w_in.T
    x_head, w1, (w_a,) = _ffn(
        x, row(ffn1_norm), ffn1_w_gate, ffn1_w_up, ffn1_w_down, row(final_gain),
        final_norm=False, name="ffn_head", tiles=(0, 1), tf=FFN_HEAD_TF,
        side_casts=((w_in_t, MAIN_COLS, GLA_GATE_RANK, d),))
    later_weights = (ffn2_w_gate, ffn2_w_up, ffn2_w_down, w_out, xattn_w_q, xattn_w_kv, xattn_w_o)
    side_casts = tuple((w, 0, *w.shape) for w in later_weights) + ((w_in_t, 0, MAIN_COLS, d),)
    x_tail, _, (w2g, w2u, w2d, w_o_mix, w_q, w_kv, w_o_att, w_main) = _ffn(
        x, row(ffn1_norm), *w1, row(final_gain), final_norm=False, name="ffn",
        tiles=(1, n_tiles - 1), side_casts=side_casts)

    w_a2 = jnp.pad(gla_w_a2, ((0, LANES - GLA_GATE_RANK), (0, 0)))
    x = _mixer(x_head, x_tail, b, row(mix_norm), w_main, w_a, bf(w_a2), row(gla_b_a),
               bf(pool_w), row(pool_scale), row(gla_head_norm), w_o_mix)

    mb, ml, _ = mem.shape
    kv = _mem_kv(mem.reshape(mb * ml, d), row(mem_norm), w_kv).reshape(mb, ml, 2 * d)
    x = _xattn(x, row(xattn_norm), w_q, kv, w_o_att)

    x, _, _ = _ffn(x.reshape(b * s, d), row(ffn2_norm), w2g, w2u, w2d, row(final_gain),
                   final_norm=final_norm, name="ffn_final" if final_norm else "ffn_tail")
    return x.reshape(b, s, d)


def kernel(x, mem, ffn1_norm, ffn1_w_gate, ffn1_w_up, ffn1_w_down, mix_norm, w_in, pool_w, pool_scale, gla_w_a2, gla_b_a, gla_head_norm, w_out, xattn_norm, mem_norm, xattn_w_q, xattn_w_kv, xattn_w_o, ffn2_norm, ffn2_w_gate, ffn2_w_up, ffn2_w_down, final_norm):
    depth = ffn1_norm.shape[0]
    for l in range(depth):
        x = _layer(x, mem, ffn1_norm[l], ffn1_w_gate[l], ffn1_w_up[l], ffn1_w_down[l], mix_norm[l],
                   w_in[l], pool_w[l], pool_scale[l], gla_w_a2[l], gla_b_a[l], gla_head_norm[l],
                   w_out[l], xattn_norm[l], mem_norm[l], xattn_w_q[l], xattn_w_kv[l], xattn_w_o[l],
                   ffn2_norm[l], ffn2_w_gate[l], ffn2_w_up[l], ffn2_w_down[l], final_norm,
                   final_norm=(l == depth - 1))
    return x
```
